```python
import jax, jax.numpy as jnp
from jax import lax
import numpy as np

D_MODEL = 1024
BATCH = 2
SEQ = 16384
DEPTH = 2

CHUNK = 64
EPS = 1e-6
ROPE_BASE = 10000.0
RET_HEADS = 4
RET_DK = 128
RET_DV = 128
RET_WIDTH = RET_HEADS * RET_DV
CONV_CH = 512
CONV_GROUPS = 8
CONV_K = 31
SG_CH = 512
SG_GROUPS = 4
SG_BLOCK = 128
SC_CH = 512
SC_GROUPS = 8
SC_K = 3
D_FF = 2816
N_EXPERTS = 8
TOP_K = 2
D_EXPERT = 3584
MOE_BLOCK = 256
EVEN_IN = 2 * RET_HEADS * RET_DK + 2 * RET_WIDTH + 2 * CONV_CH
ODD_IN = 2 * SG_CH + 3 * SC_CH
N_EVEN = (DEPTH + 1) // 2
N_ODD = DEPTH // 2

kernel_name = 'hybrid_retention_conformer_gmlp_shortconv_moe'


def rmsnorm(x, g):
    x32 = x.astype(jnp.float32)
    y = x32 * lax.rsqrt(jnp.mean(x32 * x32, axis=-1, keepdims=True) + EPS)
    return y.astype(x.dtype) * g


def layernorm(x, g, b):
    x32 = x.astype(jnp.float32)
    mu = jnp.mean(x32, axis=-1, keepdims=True)
    var = jnp.mean(jnp.square(x32 - mu), axis=-1, keepdims=True)
    y = (x32 - mu) * lax.rsqrt(var + EPS)
    return y.astype(x.dtype) * g + b


def rotary(x, pos):
    half = x.shape[-1] // 2
    freqs = ROPE_BASE ** (-jnp.arange(half, dtype=jnp.float32) / half)
    ang = pos[:, None] * freqs[None, :]
    cos = jnp.cos(ang)[None, :, None, :]
    sin = jnp.sin(ang)[None, :, None, :]
    x32 = x.astype(jnp.float32)
    x1, x2 = x32[..., :half], x32[..., half:]
    return jnp.concatenate([x1 * cos - x2 * sin, x2 * cos + x1 * sin], axis=-1)


def causal_depthwise_conv(x, w):
    k = w.shape[0]
    return lax.conv_general_dilated(
        x, w[:, None, :].astype(x.dtype), window_strides=(1,), padding=[(k - 1, 0)],
        dimension_numbers=('NWC', 'WIO', 'NWC'), feature_group_count=x.shape[-1])


def chunk_retention(q, k, v):
    bn, s, h, dk = q.shape
    dv = v.shape[-1]
    n_chunks = s // CHUNK

    def to_chunks(t):
        return t.reshape(bn, n_chunks, CHUNK, h, t.shape[-1]).transpose(1, 0, 3, 2, 4)

    log_g = jnp.log1p(-jnp.power(2.0, -5.0 - jnp.arange(h, dtype=jnp.float32)))
    idx = jnp.arange(CHUNK, dtype=jnp.float32)
    intra = jnp.exp(log_g[:, None, None] * jnp.abs(idx[:, None] - idx[None, :]))
    q_dec = jnp.exp(log_g[:, None] * (idx + 1.0))[:, :, None]
    k_dec = jnp.exp(log_g[:, None] * (CHUNK - 1.0 - idx))[:, :, None]
    c_dec = jnp.exp(log_g * CHUNK)[:, None, None]

    def step(state, inp):
        qc, kc, vc = inp
        scores = jnp.einsum('bhcd,bhld->bhcl', qc, kc) * intra
        out = (jnp.einsum('bhcl,bhle->bhce', scores, vc)
               + jnp.einsum('bhcd,bhde->bhce', qc * q_dec, state))
        state = c_dec * state + jnp.einsum('bhld,bhle->bhde', kc * k_dec, vc)
        return state, out

    init = jnp.zeros((bn, h, dk, dv), jnp.float32)
    _, o = lax.scan(step, init, (to_chunks(q), to_chunks(k), to_chunks(v)))
    return o.transpose(1, 0, 3, 2, 4).reshape(bn, s, h, dv)


def even_mixer(hn, w_in, ret_gn, dw_w, dw_b, cv_ln_g, cv_ln_b, w_out, pos):
    bn, s, _ = hn.shape
    nq = RET_HEADS * RET_DK
    z = hn @ w_in
    q, k, v, g, a, b = jnp.split(
        z, [nq, 2 * nq, 2 * nq + RET_WIDTH, 2 * nq + 2 * RET_WIDTH,
            2 * nq + 2 * RET_WIDTH + CONV_CH], axis=-1)
    q = rotary(q.reshape(bn, s, RET_HEADS, RET_DK), pos) * (RET_DK ** -0.5)
    k = rotary(k.reshape(bn, s, RET_HEADS, RET_DK), pos)
    v = v.reshape(bn, s, RET_HEADS, RET_DV).astype(jnp.float32)
    r = chunk_retention(q, k, v)
    mu = jnp.mean(r, axis=-1, keepdims=True)
    var = jnp.mean(jnp.square(r - mu), axis=-1, keepdims=True)
    r = ((r - mu) * lax.rsqrt(var + EPS)).reshape(bn, s, RET_WIDTH).astype(hn.dtype) * ret_gn
    y_ret = jax.nn.silu(g) * r
    c = a * jax.nn.sigmoid(b)
    c = causal_depthwise_conv(c, dw_w) + dw_b
    y_cv = jax.nn.silu(layernorm(c, cv_ln_g, cv_ln_b))
    return jnp.concatenate([y_ret, y_cv], axis=-1) @ w_out


def odd_mixer(hn, w_in, sg_ln_g, sg_ln_b, sg_w, sg_b, sc_w, w_out):
    bn, s, _ = hn.shape
    z = hn @ w_in
    zs, bg, cg, hv = jnp.split(z, [2 * SG_CH, 2 * SG_CH + SC_CH, 2 * SG_CH + 2 * SC_CH], axis=-1)
    zs = jax.nn.gelu(zs)
    u, v = jnp.split(zs, 2, axis=-1)
    v = layernorm(v, sg_ln_g, sg_ln_b)
    n_blk = s // SG_BLOCK
    vb = v.reshape(bn, n_blk, SG_BLOCK, SG_GROUPS, SG_CH // SG_GROUPS)
    pchunk = jnp.arange(SG_BLOCK) // CHUNK
    mask = (pchunk[:, None] >= pchunk[None, :]).astype(sg_w.dtype)
    w_s = sg_w * mask[None]
    sp = jnp.einsum('gpq,bmqgc->bmpgc', w_s, vb) + sg_b.T[None, None, :, :, None]
    y_sg = u * sp.reshape(bn, s, SG_CH)
    y_sc = bg * causal_depthwise_conv(cg * hv, sc_w)
    return jnp.concatenate([y_sg, y_sc], axis=-1) @ w_out


def swiglu(hn, w_gate, w_up, w_down):
    return (jax.nn.silu(hn @ w_gate) * (hn @ w_up)) @ w_down


def moe_swiglu(hn, router, e_gate, e_up, e_down):
    bn, s, d = hn.shape
    t = bn * s
    a = t * TOP_K
    xt = hn.reshape(t, d)
    logits = (xt @ router).astype(jnp.float32)
    top_v, top_i = lax.top_k(logits, TOP_K)
    gates = jax.nn.softmax(top_v, axis=-1)
    e_flat = top_i.reshape(-1)
    tok_flat = jnp.repeat(jnp.arange(t, dtype=jnp.int32), TOP_K)
    w_flat = gates.reshape(-1)
    order = jnp.argsort(e_flat)
    se, stok, sw = e_flat[order], tok_flat[order], w_flat[order]
    counts = jnp.bincount(e_flat, length=N_EXPERTS)
    start = jnp.cumsum(counts) - counts
    pcounts = (counts + MOE_BLOCK - 1) // MOE_BLOCK * MOE_BLOCK
    pend = jnp.cumsum(pcounts)
    pstart = pend - pcounts
    dest = pstart[se] + jnp.arange(a) - start[se]
    rows = ((a + MOE_BLOCK - 1) // MOE_BLOCK) * MOE_BLOCK + N_EXPERTS * MOE_BLOCK
    n_blocks = rows // MOE_BLOCK
    row_tok = jnp.zeros((rows,), jnp.int32).at[dest].set(stok)
    row_w = jnp.zeros((rows,), jnp.float32).at[dest].set(sw)
    blk_e = jnp.minimum(jnp.searchsorted(pend, jnp.arange(n_blocks) * MOE_BLOCK, side='right'),
                        N_EXPERTS - 1)
    xg = xt[row_tok].reshape(n_blocks, MOE_BLOCK, d)

    def expert_block(args):
        xb, e = args
        hb = jax.nn.silu(xb @ e_gate[e]) * (xb @ e_up[e])
        return hb @ e_down[e]

    yb = lax.map(expert_block, (xg, blk_e)).reshape(rows, d)
    yb = yb * row_w[:, None].astype(hn.dtype)
    out = jnp.zeros((t, d), hn.dtype).at[row_tok].add(yb)
    return out.reshape(bn, s, d)


def setup_inputs(seed: int = 0) -> dict:
    key = jax.random.key(seed)
    ks = iter(jax.random.split(key, 40))

    def dense(shape, fan_in, scale=1.0):
        return jax.random.normal(next(ks), shape, jnp.float32) * (scale * fan_in ** -0.5)

    def gain(shape):
        return 1.0 + 0.02 * jax.random.normal(next(ks), shape, jnp.float32)

    def bias(shape, scale=0.02):
        return scale * jax.random.normal(next(ks), shape, jnp.float32)

    ne, no = N_EVEN, N_ODD
    return {
        'x': jax.random.normal(next(ks), (BATCH, SEQ, D_MODEL), jnp.float32),
        'e_norm_mix': gain((ne, D_MODEL)),
        'e_w_in': dense((ne, D_MODEL, EVEN_IN), D_MODEL),
        'e_ret_gn': gain((ne, RET_WIDTH)),
        'e_dw_w': dense((ne, CONV_K, CONV_CH), CONV_K),
        'e_dw_b': bias((ne, CONV_CH)),
        'e_cv_ln_g': gain((ne, CONV_CH)),
        'e_cv_ln_b': bias((ne, CONV_CH)),
        'e_w_out': dense((ne, RET_WIDTH + CONV_CH, D_MODEL), RET_WIDTH + CONV_CH),
        'e_norm_ffn': gain((ne, D_MODEL)),
        'e_ffn_gate': dense((ne, D_MODEL, D_FF), D_MODEL),
        'e_ffn_up': dense((ne, D_MODEL, D_FF), D_MODEL),
        'e_ffn_down': dense((ne, D_FF, D_MODEL), D_FF),
        'o_norm_mix': gain((no, D_MODEL)),
        'o_w_in': dense((no, D_MODEL, ODD_IN), D_MODEL),
        'o_sg_ln_g': gain((no, SG_CH)),
        'o_sg_ln_b': bias((no, SG_CH)),
        'o_sg_w': dense((no, SG_GROUPS, SG_BLOCK, SG_BLOCK), SG_BLOCK, 0.5),
        'o_sg_b': 1.0 + bias((no, SG_GROUPS, SG_BLOCK), 0.1),
        'o_sc_w': dense((no, SC_K, SC_CH), SC_K),
        'o_w_out': dense((no, SG_CH + SC_CH, D_MODEL), SG_CH + SC_CH),
        'o_norm_ffn': gain((no, D_MODEL)),
        'o_router': dense((no, D_MODEL, N_EXPERTS), D_MODEL),
        'o_exp_gate': dense((no, N_EXPERTS, D_MODEL, D_EXPERT), D_MODEL),
        'o_exp_up': dense((no, N_EXPERTS, D_MODEL, D_EXPERT), D_MODEL),
        'o_exp_down': dense((no, N_EXPERTS, D_EXPERT, D_MODEL), D_EXPERT),
        'final_norm': gain((D_MODEL,)),
    }


def reference(x, e_norm_mix, e_w_in, e_ret_gn, e_dw_w, e_dw_b, e_cv_ln_g, e_cv_ln_b, e_w_out,
              e_norm_ffn, e_ffn_gate, e_ffn_up, e_ffn_down,
              o_norm_mix, o_w_in, o_sg_ln_g, o_sg_ln_b, o_sg_w, o_sg_b, o_sc_w, o_w_out,
              o_norm_ffn, o_router, o_exp_gate, o_exp_up, o_exp_down, final_norm):
    pos = jnp.arange(x.shape[1], dtype=jnp.float32)
    h = x
    for i in range(DEPTH):
        l = i // 2
        if i % 2 == 0:
            h = h + even_mixer(rmsnorm(h, e_norm_mix[l]), e_w_in[l], e_ret_gn[l], e_dw_w[l],
                               e_dw_b[l], e_cv_ln_g[l], e_cv_ln_b[l], e_w_out[l], pos)
            h = h + swiglu(rmsnorm(h, e_norm_ffn[l]), e_ffn_gate[l], e_ffn_up[l], e_ffn_down[l])
        else:
            h = h + odd_mixer(rmsnorm(h, o_norm_mix[l]), o_w_in[l], o_sg_ln_g[l], o_sg_ln_b[l],
                              o_sg_w[l], o_sg_b[l], o_sc_w[l], o_w_out[l])
            h = h + moe_swiglu(rmsnorm(h, o_norm_ffn[l]), o_router[l], o_exp_gate[l],
                               o_exp_up[l], o_exp_down[l])
    return rmsnorm(h, final_norm)
```

```python
import functools

import jax
import jax.numpy as jnp
from jax import lax
from jax.experimental import pallas as pl
from jax.experimental.pallas import tpu as pltpu

D_MODEL = 1024
CHUNK = 64
EPS = 1e-6
ROPE_BASE = 10000.0
RET_HEADS = 4
RET_DK = 128
RET_DV = 128
RET_WIDTH = RET_HEADS * RET_DV
CONV_CH = 512
CONV_K = 31
SG_CH = 512
SG_GROUPS = 4
SG_BLOCK = 128
SC_CH = 512
SC_K = 3
D_FF = 2816
N_EXPERTS = 8
TOP_K = 2
D_EXPERT = 3584
EVEN_IN = 2 * RET_HEADS * RET_DK + 2 * RET_WIDTH + 2 * CONV_CH
ODD_IN = 2 * SG_CH + 3 * SC_CH

LANES = 128
TM = 512
RET_L = 256
CONV_HALO = 32
CONV_ROWS = 32
SC_HALO = 8
FF_CHUNK = 1408
TR = 512
TD = 1024
TC = 256
MOE_BLOCK = 256
EXP_CHUNK = 512
VMEM_LIMIT = 60 * 1024 * 1024

BF16 = jnp.bfloat16
F32 = jnp.float32


def _dot(a, b):
    return jnp.dot(a, b, preferred_element_type=F32)


def _sigmoid(x):
    return 1.0 / (1.0 + jnp.exp(-x))


def _silu(x):
    return x * _sigmoid(x)


def _rms(x, g):
    return x * lax.rsqrt(jnp.mean(x * x, axis=-1, keepdims=True) + EPS) * g


def _ln(x, g, b):
    mu = jnp.mean(x, axis=-1, keepdims=True)
    xc = x - mu
    var = jnp.mean(xc * xc, axis=-1, keepdims=True)
    return xc * lax.rsqrt(var + EPS) * g + b


def _l0_mixer_kernel(x_ref, gn_ref, win_ref, cos_ref, sin_ref, dmat_ref, qdec_ref, kdec_ref,
                     cdec_ref, retgn_ref, dww_ref, dwb_ref, lng_ref, lnb_ref, wout_ref,
                     o_ref, state_ref, cbuf_ref, z_ref, ycat_ref):
    @pl.when(pl.program_id(1) == 0)
    def _():
        state_ref[...] = jnp.zeros_like(state_ref)
        cbuf_ref[0:CONV_HALO, :] = jnp.zeros((CONV_HALO, CONV_CH), F32)

    x = x_ref[0]
    hn = _rms(x, gn_ref[...]).astype(BF16)
    z_ref[...] = _dot(hn, win_ref[...])

    nq = RET_HEADS * RET_DK
    scale = RET_DK ** -0.5
    for j in range(TM // RET_L):
        rows = slice(j * RET_L, (j + 1) * RET_L)
        cosf = cos_ref[rows, :]
        sins = sin_ref[rows, :]
        for h in range(RET_HEADS):
            cols = slice(h * RET_DK, (h + 1) * RET_DK)
            q = z_ref[rows, h * RET_DK:(h + 1) * RET_DK]
            k = z_ref[rows, nq + h * RET_DK:nq + (h + 1) * RET_DK]
            v = z_ref[rows, 2 * nq + h * RET_DV:2 * nq + (h + 1) * RET_DV].astype(BF16)
            g = z_ref[rows, 2 * nq + RET_WIDTH + h * RET_DV:2 * nq + RET_WIDTH + (h + 1) * RET_DV]
            q = (q * cosf + pltpu.roll(q, RET_DK // 2, axis=1) * sins) * scale
            k = k * cosf + pltpu.roll(k, RET_DK // 2, axis=1) * sins
            qb = q.astype(BF16)
            kb = k.astype(BF16)
            sc = lax.dot_general(qb, kb, (((1,), (1,)), ((), ())),
                                 preferred_element_type=F32) * dmat_ref[h]
            st = state_ref[h]
            o = _dot(sc.astype(BF16), v) + _dot((q * qdec_ref[h]).astype(BF16), st.astype(BF16))
            kd_t = (k * kdec_ref[h]).T.astype(BF16)
            state_ref[h] = cdec_ref[h] * st + _dot(kd_t, v)
            mu = jnp.mean(o, axis=-1, keepdims=True)
            oc = o - mu
            var = jnp.mean(oc * oc, axis=-1, keepdims=True)
            r = oc * lax.rsqrt(var + EPS) * retgn_ref[:, cols]
            ycat_ref[rows, cols] = (_silu(g) * r).astype(BF16)

    a = z_ref[:, 2 * nq + 2 * RET_WIDTH:2 * nq + 2 * RET_WIDTH + CONV_CH]
    b = z_ref[:, 2 * nq + 2 * RET_WIDTH + CONV_CH:]
    cbuf_ref[CONV_HALO:CONV_HALO + TM, :] = a * _sigmoid(b)
    first = CONV_HALO - (CONV_K - 1)
    for r0 in range(0, TM, CONV_ROWS):
        acc = jnp.zeros((CONV_ROWS, CONV_CH), F32)
        for t in range(CONV_K):
            acc = acc + cbuf_ref[r0 + first + t:r0 + first + t + CONV_ROWS, :] * dww_ref[t:t + 1, :]
        cv = _ln(acc + dwb_ref[...], lng_ref[...], lnb_ref[...])
        ycat_ref[r0:r0 + CONV_ROWS, RET_WIDTH:] = _silu(cv).astype(BF16)
    cbuf_ref[0:CONV_HALO, :] = cbuf_ref[TM:TM + CONV_HALO, :]

    o_ref[0] = x + _dot(ycat_ref[...], wout_ref[...])


def _l0_mixer(x, gn, win, cosf, sins, dmat, qdec, kdec, cdec, retgn, dww, dwb, lng, lnb, wout):
    bn, s, d = x.shape
    full = lambda shape: pl.BlockSpec(shape, lambda b, i: (0,) * len(shape))
    return pl.pallas_call(
        _l0_mixer_kernel,
        grid=(bn, s // TM),
        in_specs=[
            pl.BlockSpec((1, TM, d), lambda b, i: (b, i, 0)),
            full((1, d)),
            full((d, EVEN_IN)),
            pl.BlockSpec((TM, RET_DK), lambda b, i: (i, 0)),
            pl.BlockSpec((TM, RET_DK), lambda b, i: (i, 0)),
            full((RET_HEADS, RET_L, RET_L)),
            full((RET_HEADS, RET_L, RET_DK)),
            full((RET_HEADS, RET_L, RET_DK)),
            full((RET_HEADS, 1, RET_DV)),
            full((1, RET_WIDTH)),
            full((CONV_K, CONV_CH)),
            full((1, CONV_CH)),
            full((1, CONV_CH)),
            full((1, CONV_CH)),
            full((RET_WIDTH + CONV_CH, d)),
        ],
        out_specs=pl.BlockSpec((1, TM, d), lambda b, i: (b, i, 0)),
        out_shape=jax.ShapeDtypeStruct((bn, s, d), F32),
        scratch_shapes=[
            pltpu.VMEM((RET_HEADS, RET_DK, RET_DV), F32),
            pltpu.VMEM((TM + CONV_HALO, CONV_CH), F32),
            pltpu.VMEM((TM, EVEN_IN), F32),
            pltpu.VMEM((TM, RET_WIDTH + CONV_CH), BF16),
        ],
        compiler_params=pltpu.CompilerParams(
            dimension_semantics=("arbitrary", "arbitrary"), vmem_limit_bytes=VMEM_LIMIT),
        name="l0_mixer",
    )(x, gn, win, cosf, sins, dmat, qdec, kdec, cdec, retgn, dww, dwb, lng, lnb, wout)


def _ffn_kernel(h_ref, gn_ref, wg_ref, wu_ref, wd_ref, o_ref):
    h = h_ref[...]
    hn = _rms(h, gn_ref[...]).astype(BF16)
    acc = h
    for c in range(D_FF // FF_CHUNK):
        cols = slice(c * FF_CHUNK, (c + 1) * FF_CHUNK)
        g = _dot(hn, wg_ref[:, cols])
        u = _dot(hn, wu_ref[:, cols])
        acc = acc + _dot((_silu(g) * u).astype(BF16), wd_ref[cols, :])
    o_ref[...] = acc


def _ffn(h, gn, wg, wu, wd):
    t, d = h.shape
    full = lambda shape: pl.BlockSpec(shape, lambda i: (0,) * len(shape))
    return pl.pallas_call(
        _ffn_kernel,
        grid=(t // TM,),
        in_specs=[
            pl.BlockSpec((TM, d), lambda i: (i, 0)),
            full((1, d)),
            full((d, D_FF)),
            full((d, D_FF)),
            full((D_FF, d)),
        ],
        out_specs=pl.BlockSpec((TM, d), lambda i: (i, 0)),
        out_shape=jax.ShapeDtypeStruct((t, d), F32),
        compiler_params=pltpu.CompilerParams(
            dimension_semantics=("arbitrary",), vmem_limit_bytes=VMEM_LIMIT),
        name="ffn_swiglu",
    )(h, gn, wg, wu, wd)


def _l1_mixer_kernel(x_ref, gn_ref, win_ref, lng_ref, lnb_ref, sgw_ref, sgb_ref, scw_ref, wout_ref,
                     o_ref, pbuf_ref, z_ref, ycat_ref):
    @pl.when(pl.program_id(1) == 0)
    def _():
        pbuf_ref[0:SC_HALO, :] = jnp.zeros((SC_HALO, SC_CH), F32)

    x = x_ref[0]
    hn = _rms(x, gn_ref[...]).astype(BF16)
    z_ref[...] = _dot(hn, win_ref[...])

    vn = _ln(jax.nn.gelu(z_ref[:, SG_CH:2 * SG_CH]), lng_ref[...], lnb_ref[...]).astype(BF16)
    gw = SG_CH // SG_GROUPS
    pc = lax.broadcasted_iota(jnp.int32, (SG_BLOCK, SG_BLOCK), 0) // CHUNK
    qc = lax.broadcasted_iota(jnp.int32, (SG_BLOCK, SG_BLOCK), 1) // CHUNK
    for g in range(SG_GROUPS):
        cols = slice(g * gw, (g + 1) * gw)
        ws = jnp.where(pc >= qc, sgw_ref[g], 0.0).astype(BF16)
        for m in range(TM // SG_BLOCK):
            rows = slice(m * SG_BLOCK, (m + 1) * SG_BLOCK)
            sp = _dot(ws, vn[rows, cols]) + sgb_ref[g]
            u = jax.nn.gelu(z_ref[rows, g * gw:(g + 1) * gw])
            ycat_ref[rows, cols] = (u * sp).astype(BF16)

    base = 2 * SG_CH
    bg = z_ref[:, base:base + SC_CH]
    cg = z_ref[:, base + SC_CH:base + 2 * SC_CH]
    hv = z_ref[:, base + 2 * SC_CH:]
    pbuf_ref[SC_HALO:SC_HALO + TM, :] = cg * hv
    first = SC_HALO - (SC_K - 1)
    conv = jnp.zeros((TM, SC_CH), F32)
    for t in range(SC_K):
        conv = conv + pbuf_ref[first + t:first + t + TM, :] * scw_ref[t:t + 1, :]
    ycat_ref[:, SG_CH:] = (bg * conv).astype(BF16)
    pbuf_ref[0:SC_HALO, :] = pbuf_ref[TM:TM + SC_HALO, :]

    o_ref[0] = x + _dot(ycat_ref[...], wout_ref[...])


def _l1_mixer(x, gn, win, lng, lnb, sgw, sgb, scw, wout):
    bn, s, d = x.shape
    full = lambda shape: pl.BlockSpec(shape, lambda b, i: (0,) * len(shape))
    return pl.pallas_call(
        _l1_mixer_kernel,
        grid=(bn, s // TM),
        in_specs=[
            pl.BlockSpec((1, TM, d), lambda b, i: (b, i, 0)),
            full((1, d)),
            full((d, ODD_IN)),
            full((1, SG_CH)),
            full((1, SG_CH)),
            full((SG_GROUPS, SG_BLOCK, SG_BLOCK)),
            full((SG_GROUPS, SG_BLOCK, SG_CH // SG_GROUPS)),
            full((SC_K, SC_CH)),
            full((SG_CH + SC_CH, d)),
        ],
        out_specs=pl.BlockSpec((1, TM, d), lambda b, i: (b, i, 0)),
        out_shape=jax.ShapeDtypeStruct((bn, s, d), F32),
        scratch_shapes=[
            pltpu.VMEM((TM + SC_HALO, SC_CH), F32),
            pltpu.VMEM((TM, ODD_IN), F32),
            pltpu.VMEM((TM, SG_CH + SC_CH), BF16),
        ],
        compiler_params=pltpu.CompilerParams(
            dimension_semantics=("arbitrary", "arbitrary"), vmem_limit_bytes=VMEM_LIMIT),
        name="l1_mixer",
    )(x, gn, win, lng, lnb, sgw, sgb, scw, wout)


def _router_kernel(h_ref, gn_ref, rw_ref, hn_ref, meta_ref, gate_ref, cnt_ref):
    @pl.when(pl.program_id(0) == 0)
    def _():
        cnt_ref[...] = jnp.zeros_like(cnt_ref)

    hn = _rms(h_ref[...], gn_ref[...])
    hn_ref[...] = hn
    hn_hi = hn.astype(BF16)
    hn_lo = (hn - hn_hi.astype(F32)).astype(BF16)
    rw = rw_ref[...]
    rw_hi = rw.astype(BF16)
    rw_lo = (rw - rw_hi.astype(F32)).astype(BF16)
    logits = _dot(hn_hi, rw_hi) + (_dot(hn_hi, rw_lo) + _dot(hn_lo, rw_hi))

    lane = lax.broadcasted_iota(jnp.int32, (TR, LANES), 1)
    neg = jnp.float32(-jnp.inf)
    l1 = jnp.where(lane < N_EXPERTS, logits, neg)
    m1 = jnp.max(l1, axis=-1, keepdims=True)
    i1 = jnp.min(jnp.where(l1 == m1, lane, LANES), axis=-1, keepdims=True)
    l2 = jnp.where(lane == i1, neg, l1)
    m2 = jnp.max(l2, axis=-1, keepdims=True)
    i2 = jnp.min(jnp.where(l2 == m2, lane, LANES), axis=-1, keepdims=True)
    e21 = jnp.exp(m2 - m1)
    g1 = 1.0 / (1.0 + e21)
    g2 = e21 / (1.0 + e21)

    oh1 = lane == i1
    oh2 = lane == i2
    both = jnp.where(oh1 | oh2, 1.0, 0.0)
    row = lax.broadcasted_iota(jnp.int32, (TR, TR), 0)
    col = lax.broadcasted_iota(jnp.int32, (TR, TR), 1)
    lower = jnp.where(row > col, 1.0, 0.0).astype(BF16)
    prefix = _dot(lower, both.astype(BF16)) + cnt_ref[...]
    r1 = jnp.sum(jnp.where(oh1, prefix, 0.0), axis=-1, keepdims=True)
    r2 = jnp.sum(jnp.where(oh2, prefix, 0.0), axis=-1, keepdims=True)
    cnt_ref[...] = cnt_ref[...] + jnp.sum(both, axis=0, keepdims=True)

    r1i = r1.astype(jnp.int32)
    r2i = r2.astype(jnp.int32)
    meta = jnp.where(lane == 0, i1, jnp.where(lane == 1, i2, jnp.where(lane == 2, r1i, r2i)))
    meta_ref[...] = meta[:, 0:4]
    gate_ref[...] = jnp.where(lane == 0, g1, g2)[:, 0:2]


def _router(h, gn, rw):
    t, d = h.shape
    return pl.pallas_call(
        _router_kernel,
        grid=(t // TR,),
        in_specs=[
            pl.BlockSpec((TR, d), lambda i: (i, 0)),
            pl.BlockSpec((1, d), lambda i: (0, 0)),
            pl.BlockSpec((d, LANES), lambda i: (0, 0)),
        ],
        out_specs=[
            pl.BlockSpec((TR, d), lambda i: (i, 0)),
            pl.BlockSpec((TR, 4), lambda i: (i, 0)),
            pl.BlockSpec((TR, 2), lambda i: (i, 0)),
            pl.BlockSpec((1, LANES), lambda i: (0, 0)),
        ],
        out_shape=[
            jax.ShapeDtypeStruct((t, d), F32),
            jax.ShapeDtypeStruct((t, 4), jnp.int32),
            jax.ShapeDtypeStruct((t, 2), F32),
            jax.ShapeDtypeStruct((1, LANES), F32),
        ],
        compiler_params=pltpu.CompilerParams(
            dimension_semantics=("arbitrary",), vmem_limit_bytes=VMEM_LIMIT),
        name="moe_router",
    )(h, gn, rw)


def _row_copy(src_hbm, src_row, dst_hbm, dst_row, sem):
    return pltpu.make_async_copy(src_hbm.at[pl.ds(src_row, 1)], dst_hbm.at[pl.ds(dst_row, 1)], sem)


def _dispatch_kernel(dest_ref, hn_hbm, xg_in_hbm, xg_hbm, sem):
    del xg_in_hbm
    tok0 = pl.program_id(0) * TD

    def issue(t, carry):
        for k in range(TOP_K):
            _row_copy(hn_hbm, tok0 + t, xg_hbm, dest_ref[k, t], sem).start()
        return carry

    lax.fori_loop(0, TD, issue, 0)

    def drain(t, carry):
        for k in range(TOP_K):
            _row_copy(hn_hbm, tok0 + t, xg_hbm, dest_ref[k, t], sem).wait()
        return carry

    lax.fori_loop(0, TD, drain, 0)


def _dispatch(dest_t, hn, xg_zero):
    t, d = hn.shape
    return pl.pallas_call(
        _dispatch_kernel,
        grid=(t // TD,),
        in_specs=[
            pl.BlockSpec((TOP_K, TD), lambda i: (0, i), memory_space=pltpu.SMEM),
            pl.BlockSpec(memory_space=pl.ANY),
            pl.BlockSpec(memory_space=pl.ANY),
        ],
        out_specs=pl.BlockSpec(memory_space=pl.ANY),
        out_shape=jax.ShapeDtypeStruct(xg_zero.shape, F32),
        scratch_shapes=[pltpu.SemaphoreType.DMA],
        input_output_aliases={2: 0},
        compiler_params=pltpu.CompilerParams(
            dimension_semantics=("arbitrary",), has_side_effects=True),
        name="moe_dispatch",
    )(dest_t, hn, xg_zero)


def _expert_kernel(blk_e_ref, nact_ref, x_ref, wg_ref, wu_ref, wd_ref, o_ref):
    del blk_e_ref

    @pl.when(pl.program_id(0) < nact_ref[0])
    def _():
        xb = x_ref[...].astype(BF16)
        acc = jnp.zeros((MOE_BLOCK, D_MODEL), F32)
        for c in range(D_EXPERT // EXP_CHUNK):
            cols = slice(c * EXP_CHUNK, (c + 1) * EXP_CHUNK)
            g = _dot(xb, wg_ref[:, cols])
            u = _dot(xb, wu_ref[:, cols])
            acc = acc + _dot((_silu(g) * u).astype(BF16), wd_ref[cols, :])
        o_ref[...] = acc

    @pl.when(pl.program_id(0) >= nact_ref[0])
    def _():
        o_ref[...] = jnp.zeros_like(o_ref)


def _experts(blk_e, nact, xg, wg, wu, wd):
    rows, d = xg.shape
    nblk = rows // MOE_BLOCK

    def row_map(i, be, na):
        return (jnp.minimum(i, na[0] - 1), 0)

    def w_map(i, be, na):
        return (be[i], 0, 0)

    grid_spec = pltpu.PrefetchScalarGridSpec(
        num_scalar_prefetch=2,
        grid=(nblk,),
        in_specs=[
            pl.BlockSpec((MOE_BLOCK, d), row_map),
            pl.BlockSpec((None, d, D_EXPERT), w_map),
            pl.BlockSpec((None, d, D_EXPERT), w_map),
            pl.BlockSpec((None, D_EXPERT, d), w_map),
        ],
        out_specs=pl.BlockSpec((MOE_BLOCK, d), lambda i, be, na: (i, 0)),
    )
    return pl.pallas_call(
        _expert_kernel,
        grid_spec=grid_spec,
        out_shape=jax.ShapeDtypeStruct((rows, d), F32),
        compiler_params=pltpu.CompilerParams(
            dimension_semantics=("arbitrary",), vmem_limit_bytes=VMEM_LIMIT),
        name="moe_experts",
    )(blk_e, nact, xg, wg, wu, wd)


def _combine_kernel(dest_ref, h_ref, gate_ref, fn_ref, yb_hbm, o_ref, buf_ref, sem):
    def issue(t, carry):
        for k in range(TOP_K):
            pltpu.make_async_copy(yb_hbm.at[pl.ds(dest_ref[k, t], 1)],
                                  buf_ref.at[k, pl.ds(t, 1)], sem).start()
        return carry

    lax.fori_loop(0, TC, issue, 0)

    def drain(t, carry):
        for k in range(TOP_K):
            pltpu.make_async_copy(yb_hbm.at[pl.ds(dest_ref[k, t], 1)],
                                  buf_ref.at[k, pl.ds(t, 1)], sem).wait()
        return carry

    lax.fori_loop(0, TC, drain, 0)

    gates = gate_ref[...]
    h = h_ref[...] + (gates[:, 0:1] * buf_ref[0] + gates[:, 1:2] * buf_ref[1])
    o_ref[...] = _rms(h, fn_ref[...])


def _combine(dest_t, h, gates, fn, yb):
    t, d = h.shape
    return pl.pallas_call(
        _combine_kernel,
        grid=(t // TC,),
        in_specs=[
            pl.BlockSpec((TOP_K, TC), lambda i: (0, i), memory_space=pltpu.SMEM),
            pl.BlockSpec((TC, d), lambda i: (i, 0)),
            pl.BlockSpec((TC, TOP_K), lambda i: (i, 0)),
            pl.BlockSpec((1, d), lambda i: (0, 0)),
            pl.BlockSpec(memory_space=pl.ANY),
        ],
        out_specs=pl.BlockSpec((TC, d), lambda i: (i, 0)),
        out_shape=jax.ShapeDtypeStruct((t, d), F32),
        scratch_shapes=[pltpu.VMEM((TOP_K, TC, d), F32), pltpu.SemaphoreType.DMA],
        compiler_params=pltpu.CompilerParams(
            dimension_semantics=("arbitrary",), vmem_limit_bytes=VMEM_LIMIT),
        name="moe_combine",
    )(dest_t, h, gates, fn, yb)


def _retention_tables(s):
    half = RET_DK // 2
    pos = jnp.arange(s, dtype=F32)
    freqs = ROPE_BASE ** (-jnp.arange(half, dtype=F32) / half)
    ang = pos[:, None] * freqs[None, :]
    cos, sin = jnp.cos(ang), jnp.sin(ang)
    cosf = jnp.concatenate([cos, cos], axis=-1)
    sins = jnp.concatenate([-sin, sin], axis=-1)
    log_g = jnp.log1p(-jnp.power(2.0, -5.0 - jnp.arange(RET_HEADS, dtype=F32)))
    idx = jnp.arange(RET_L, dtype=F32)
    ck = jnp.arange(RET_L) // CHUNK
    mask = (ck[:, None] >= ck[None, :]).astype(F32)
    dmat = jnp.exp(log_g[:, None, None] * jnp.abs(idx[:, None] - idx[None, :])) * mask[None]
    qdec = jnp.exp(log_g[:, None] * (idx + 1.0))
    kdec = jnp.exp(log_g[:, None] * (RET_L - 1.0 - idx))
    cdec = jnp.exp(log_g * RET_L)
    qdec = jnp.broadcast_to(qdec[:, :, None], (RET_HEADS, RET_L, RET_DK))
    kdec = jnp.broadcast_to(kdec[:, :, None], (RET_HEADS, RET_L, RET_DK))
    cdec = jnp.broadcast_to(cdec[:, None, None], (RET_HEADS, 1, RET_DV))
    return cosf, sins, dmat, qdec, kdec, cdec


def _moe_plan(meta, counts):
    t = meta.shape[0]
    counts = counts.astype(jnp.int32)
    pcounts = (counts + MOE_BLOCK - 1) // MOE_BLOCK * MOE_BLOCK
    pend = jnp.cumsum(pcounts)
    pstart = pend - pcounts
    dest = pstart[meta[:, 0:TOP_K]] + meta[:, TOP_K:2 * TOP_K]
    rows = (t * TOP_K + MOE_BLOCK - 1) // MOE_BLOCK * MOE_BLOCK + N_EXPERTS * MOE_BLOCK
    nblk = rows // MOE_BLOCK
    nact = pend[-1] // MOE_BLOCK
    blk = jnp.minimum(jnp.arange(nblk, dtype=jnp.int32), nact - 1)
    blk_e = jnp.minimum(jnp.searchsorted(pend, blk * MOE_BLOCK, side='right'), N_EXPERTS - 1)
    return dest.T.astype(jnp.int32), blk_e.astype(jnp.int32), nact.reshape(1).astype(jnp.int32), rows


def kernel(x, e_norm_mix, e_w_in, e_ret_gn, e_dw_w, e_dw_b, e_cv_ln_g, e_cv_ln_b, e_w_out, e_norm_ffn, e_ffn_gate, e_ffn_up, e_ffn_down, o_norm_mix, o_w_in, o_sg_ln_g, o_sg_ln_b, o_sg_w, o_sg_b, o_sc_w, o_w_out, o_norm_ffn, o_router, o_exp_gate, o_exp_up, o_exp_down, final_norm):
    bn, s, d = x.shape
    t = bn * s
    row = lambda v: v.reshape(1, -1)
    cosf, sins, dmat, qdec, kdec, cdec = _retention_tables(s)

    h = _l0_mixer(x, row(e_norm_mix[0]), e_w_in[0].astype(BF16), cosf, sins, dmat, qdec, kdec, cdec,
                  row(e_ret_gn[0]), e_dw_w[0], row(e_dw_b[0]), row(e_cv_ln_g[0]), row(e_cv_ln_b[0]),
                  e_w_out[0].astype(BF16))
    h = _ffn(h.reshape(t, d), row(e_norm_ffn[0]), e_ffn_gate[0].astype(BF16),
             e_ffn_up[0].astype(BF16), e_ffn_down[0].astype(BF16))

    sgb = jnp.broadcast_to(o_sg_b[0][:, :, None], (SG_GROUPS, SG_BLOCK, SG_CH // SG_GROUPS))
    h = _l1_mixer(h.reshape(bn, s, d), row(o_norm_mix[0]), o_w_in[0].astype(BF16),
                  row(o_sg_ln_g[0]), row(o_sg_ln_b[0]), o_sg_w[0], sgb, o_sc_w[0],
                  o_w_out[0].astype(BF16)).reshape(t, d)

    rw = jnp.pad(o_router[0], ((0, 0), (0, LANES - N_EXPERTS)))
    hn, meta, gates, counts = _router(h, row(o_norm_ffn[0]), rw)
    dest_t, blk_e, nact, rows = _moe_plan(meta, counts[0, :N_EXPERTS])
    xg = _dispatch(dest_t, hn, jnp.zeros((rows, d), F32))
    yb = _experts(blk_e, nact, xg, o_exp_gate[0].astype(BF16), o_exp_up[0].astype(BF16),
                  o_exp_down[0].astype(BF16))
    out = _combine(dest_t, h, gates, row(final_norm), yb)
    return out.reshape(bn, s, d)
```

```python
import functools

import jax
import jax.numpy as jnp
from jax import lax
from jax.experimental import pallas as pl
from jax.experimental.pallas import tpu as pltpu

D_MODEL = 1024
CHUNK = 64
EPS = 1e-6
ROPE_BASE = 10000.0
RET_HEADS = 4
RET_DK = 128
RET_DV = 128
RET_WIDTH = RET_HEADS * RET_DV
CONV_CH = 512
CONV_K = 31
SG_CH = 512
SG_GROUPS = 4
SG_BLOCK = 128
SC_CH = 512
SC_K = 3
D_FF = 2816
N_EXPERTS = 8
TOP_K = 2
D_EXPERT = 3584
EVEN_IN = 2 * RET_HEADS * RET_DK + 2 * RET_WIDTH + 2 * CONV_CH
ODD_IN = 2 * SG_CH + 3 * SC_CH

LANES = 128
TM = 512
RET_L = 256
CONV_HALO = 32
CONV_ROWS = 32
SC_HALO = 8
FF_CHUNK = 1408
TR = 512
TD = 512
DMA_UNROLL = 8
TC = 256
MOE_BLOCK = 256
EXP_CHUNK = 512
VMEM_LIMIT = 60 * 1024 * 1024

BF16 = jnp.bfloat16
F32 = jnp.float32


def _dot(a, b):
    return jnp.dot(a, b, preferred_element_type=F32)


def _sigmoid(x):
    return 1.0 / (1.0 + jnp.exp(-x))


def _silu(x):
    return x * _sigmoid(x)


def _rms(x, g):
    return x * lax.rsqrt(jnp.mean(x * x, axis=-1, keepdims=True) + EPS) * g


def _ln(x, g, b):
    mu = jnp.mean(x, axis=-1, keepdims=True)
    xc = x - mu
    var = jnp.mean(xc * xc, axis=-1, keepdims=True)
    return xc * lax.rsqrt(var + EPS) * g + b


def _l0_mixer_kernel(x_ref, gn_ref, win_ref, cos_ref, sin_ref, dmat_ref, qdec_ref, kdec_ref,
                     cdec_ref, retgn_ref, dww_ref, dwb_ref, lng_ref, lnb_ref, wout_ref,
                     o_ref, state_ref, cbuf_ref, z_ref, ycat_ref):
    @pl.when(pl.program_id(1) == 0)
    def _():
        state_ref[...] = jnp.zeros_like(state_ref)
        cbuf_ref[0:CONV_HALO, :] = jnp.zeros((CONV_HALO, CONV_CH), F32)

    x = x_ref[0]
    hn = _rms(x, gn_ref[...]).astype(BF16)
    z_ref[...] = _dot(hn, win_ref[...])

    nq = RET_HEADS * RET_DK
    scale = RET_DK ** -0.5
    for j in range(TM // RET_L):
        rows = slice(j * RET_L, (j + 1) * RET_L)
        cosf = cos_ref[rows, :]
        sins = sin_ref[rows, :]
        for h in range(RET_HEADS):
            cols = slice(h * RET_DK, (h + 1) * RET_DK)
            q = z_ref[rows, h * RET_DK:(h + 1) * RET_DK]
            k = z_ref[rows, nq + h * RET_DK:nq + (h + 1) * RET_DK]
            v = z_ref[rows, 2 * nq + h * RET_DV:2 * nq + (h + 1) * RET_DV].astype(BF16)
            g = z_ref[rows, 2 * nq + RET_WIDTH + h * RET_DV:2 * nq + RET_WIDTH + (h + 1) * RET_DV]
            q = (q * cosf + pltpu.roll(q, RET_DK // 2, axis=1) * sins) * scale
            k = k * cosf + pltpu.roll(k, RET_DK // 2, axis=1) * sins
            qb = q.astype(BF16)
            kb = k.astype(BF16)
            sc = lax.dot_general(qb, kb, (((1,), (1,)), ((), ())),
                                 preferred_element_type=F32) * dmat_ref[h]
            st = state_ref[h]
            o = _dot(sc.astype(BF16), v) + _dot((q * qdec_ref[h]).astype(BF16), st.astype(BF16))
            kd_t = (k * kdec_ref[h]).T.astype(BF16)
            state_ref[h] = cdec_ref[h] * st + _dot(kd_t, v)
            mu = jnp.mean(o, axis=-1, keepdims=True)
            oc = o - mu
            var = jnp.mean(oc * oc, axis=-1, keepdims=True)
            r = oc * lax.rsqrt(var + EPS) * retgn_ref[:, cols]
            ycat_ref[rows, cols] = (_silu(g) * r).astype(BF16)

    a = z_ref[:, 2 * nq + 2 * RET_WIDTH:2 * nq + 2 * RET_WIDTH + CONV_CH]
    b = z_ref[:, 2 * nq + 2 * RET_WIDTH + CONV_CH:]
    cbuf_ref[CONV_HALO:CONV_HALO + TM, :] = a * _sigmoid(b)
    first = CONV_HALO - (CONV_K - 1)
    for r0 in range(0, TM, CONV_ROWS):
        acc = jnp.zeros((CONV_ROWS, CONV_CH), F32)
        for t in range(CONV_K):
            acc = acc + cbuf_ref[r0 + first + t:r0 + first + t + CONV_ROWS, :] * dww_ref[t:t + 1, :]
        cv = _ln(acc + dwb_ref[...], lng_ref[...], lnb_ref[...])
        ycat_ref[r0:r0 + CONV_ROWS, RET_WIDTH:] = _silu(cv).astype(BF16)
    cbuf_ref[0:CONV_HALO, :] = cbuf_ref[TM:TM + CONV_HALO, :]

    o_ref[0] = x + _dot(ycat_ref[...], wout_ref[...])


def _l0_mixer(x, gn, win, cosf, sins, dmat, qdec, kdec, cdec, retgn, dww, dwb, lng, lnb, wout):
    bn, s, d = x.shape
    full = lambda shape: pl.BlockSpec(shape, lambda b, i: (0,) * len(shape))
    return pl.pallas_call(
        _l0_mixer_kernel,
        grid=(bn, s // TM),
        in_specs=[
            pl.BlockSpec((1, TM, d), lambda b, i: (b, i, 0)),
            full((1, d)),
            full((d, EVEN_IN)),
            pl.BlockSpec((TM, RET_DK), lambda b, i: (i, 0)),
            pl.BlockSpec((TM, RET_DK), lambda b, i: (i, 0)),
            full((RET_HEADS, RET_L, RET_L)),
            full((RET_HEADS, RET_L, RET_DK)),
            full((RET_HEADS, RET_L, RET_DK)),
            full((RET_HEADS, 1, RET_DV)),
            full((1, RET_WIDTH)),
            full((CONV_K, CONV_CH)),
            full((1, CONV_CH)),
            full((1, CONV_CH)),
            full((1, CONV_CH)),
            full((RET_WIDTH + CONV_CH, d)),
        ],
        out_specs=pl.BlockSpec((1, TM, d), lambda b, i: (b, i, 0)),
        out_shape=jax.ShapeDtypeStruct((bn, s, d), F32),
        scratch_shapes=[
            pltpu.VMEM((RET_HEADS, RET_DK, RET_DV), F32),
            pltpu.VMEM((TM + CONV_HALO, CONV_CH), F32),
            pltpu.VMEM((TM, EVEN_IN), F32),
            pltpu.VMEM((TM, RET_WIDTH + CONV_CH), BF16),
        ],
        compiler_params=pltpu.CompilerParams(
            dimension_semantics=("arbitrary", "arbitrary"), vmem_limit_bytes=VMEM_LIMIT),
        name="l0_mixer",
    )(x, gn, win, cosf, sins, dmat, qdec, kdec, cdec, retgn, dww, dwb, lng, lnb, wout)


def _ffn_kernel(h_ref, gn_ref, wg_ref, wu_ref, wd_ref, o_ref):
    h = h_ref[...]
    hn = _rms(h, gn_ref[...]).astype(BF16)
    acc = h
    for c in range(D_FF // FF_CHUNK):
        cols = slice(c * FF_CHUNK, (c + 1) * FF_CHUNK)
        g = _dot(hn, wg_ref[:, cols])
        u = _dot(hn, wu_ref[:, cols])
        acc = acc + _dot((_silu(g) * u).astype(BF16), wd_ref[cols, :])
    o_ref[...] = acc


def _ffn(h, gn, wg, wu, wd):
    t, d = h.shape
    full = lambda shape: pl.BlockSpec(shape, lambda i: (0,) * len(shape))
    return pl.pallas_call(
        _ffn_kernel,
        grid=(t // TM,),
        in_specs=[
            pl.BlockSpec((TM, d), lambda i: (i, 0)),
            full((1, d)),
            full((d, D_FF)),
            full((d, D_FF)),
            full((D_FF, d)),
        ],
        out_specs=pl.BlockSpec((TM, d), lambda i: (i, 0)),
        out_shape=jax.ShapeDtypeStruct((t, d), F32),
        compiler_params=pltpu.CompilerParams(
            dimension_semantics=("arbitrary",), vmem_limit_bytes=VMEM_LIMIT),
        name="ffn_swiglu",
    )(h, gn, wg, wu, wd)


def _l1_mixer_kernel(x_ref, gn_ref, win_ref, lng_ref, lnb_ref, sgw_ref, sgb_ref, scw_ref, wout_ref,
                     o_ref, pbuf_ref, z_ref, ycat_ref):
    @pl.when(pl.program_id(1) == 0)
    def _():
        pbuf_ref[0:SC_HALO, :] = jnp.zeros((SC_HALO, SC_CH), F32)

    x = x_ref[0]
    hn = _rms(x, gn_ref[...]).astype(BF16)
    z_ref[...] = _dot(hn, win_ref[...])

    vn = _ln(jax.nn.gelu(z_ref[:, SG_CH:2 * SG_CH]), lng_ref[...], lnb_ref[...]).astype(BF16)
    gw = SG_CH // SG_GROUPS
    pc = lax.broadcasted_iota(jnp.int32, (SG_BLOCK, SG_BLOCK), 0) // CHUNK
    qc = lax.broadcasted_iota(jnp.int32, (SG_BLOCK, SG_BLOCK), 1) // CHUNK
    for g in range(SG_GROUPS):
        cols = slice(g * gw, (g + 1) * gw)
        ws = jnp.where(pc >= qc, sgw_ref[g], 0.0).astype(BF16)
        for m in range(TM // SG_BLOCK):
            rows = slice(m * SG_BLOCK, (m + 1) * SG_BLOCK)
            sp = _dot(ws, vn[rows, cols]) + sgb_ref[g]
            u = jax.nn.gelu(z_ref[rows, g * gw:(g + 1) * gw])
            ycat_ref[rows, cols] = (u * sp).astype(BF16)

    base = 2 * SG_CH
    bg = z_ref[:, base:base + SC_CH]
    cg = z_ref[:, base + SC_CH:base + 2 * SC_CH]
    hv = z_ref[:, base + 2 * SC_CH:]
    pbuf_ref[SC_HALO:SC_HALO + TM, :] = cg * hv
    first = SC_HALO - (SC_K - 1)
    conv = jnp.zeros((TM, SC_CH), F32)
    for t in range(SC_K):
        conv = conv + pbuf_ref[first + t:first + t + TM, :] * scw_ref[t:t + 1, :]
    ycat_ref[:, SG_CH:] = (bg * conv).astype(BF16)
    pbuf_ref[0:SC_HALO, :] = pbuf_ref[TM:TM + SC_HALO, :]

    o_ref[0] = x + _dot(ycat_ref[...], wout_ref[...])


def _l1_mixer(x, gn, win, lng, lnb, sgw, sgb, scw, wout):
    bn, s, d = x.shape
    full = lambda shape: pl.BlockSpec(shape, lambda b, i: (0,) * len(shape))
    return pl.pallas_call(
        _l1_mixer_kernel,
        grid=(bn, s // TM),
        in_specs=[
            pl.BlockSpec((1, TM, d), lambda b, i: (b, i, 0)),
            full((1, d)),
            full((d, ODD_IN)),
            full((1, SG_CH)),
            full((1, SG_CH)),
            full((SG_GROUPS, SG_BLOCK, SG_BLOCK)),
            full((SG_GROUPS, SG_BLOCK, SG_CH // SG_GROUPS)),
            full((SC_K, SC_CH)),
            full((SG_CH + SC_CH, d)),
        ],
        out_specs=pl.BlockSpec((1, TM, d), lambda b, i: (b, i, 0)),
        out_shape=jax.ShapeDtypeStruct((bn, s, d), F32),
        scratch_shapes=[
            pltpu.VMEM((TM + SC_HALO, SC_CH), F32),
            pltpu.VMEM((TM, ODD_IN), F32),
            pltpu.VMEM((TM, SG_CH + SC_CH), BF16),
        ],
        compiler_params=pltpu.CompilerParams(
            dimension_semantics=("arbitrary", "arbitrary"), vmem_limit_bytes=VMEM_LIMIT),
        name="l1_mixer",
    )(x, gn, win, lng, lnb, sgw, sgb, scw, wout)


def _router_kernel(h_ref, gn_ref, rw_ref, hn_ref, meta_ref, gate_ref, cnt_ref):
    @pl.when(pl.program_id(0) == 0)
    def _():
        cnt_ref[...] = jnp.zeros_like(cnt_ref)

    hn = _rms(h_ref[...], gn_ref[...])
    hn_ref[...] = hn
    hn_hi = hn.astype(BF16)
    hn_lo = (hn - hn_hi.astype(F32)).astype(BF16)
    rw = rw_ref[...]
    rw_hi = rw.astype(BF16)
    rw_lo = (rw - rw_hi.astype(F32)).astype(BF16)
    logits = _dot(hn_hi, rw_hi) + (_dot(hn_hi, rw_lo) + _dot(hn_lo, rw_hi))

    lane = lax.broadcasted_iota(jnp.int32, (TR, LANES), 1)
    neg = jnp.float32(-jnp.inf)
    l1 = jnp.where(lane < N_EXPERTS, logits, neg)
    m1 = jnp.max(l1, axis=-1, keepdims=True)
    i1 = jnp.min(jnp.where(l1 == m1, lane, LANES), axis=-1, keepdims=True)
    l2 = jnp.where(lane == i1, neg, l1)
    m2 = jnp.max(l2, axis=-1, keepdims=True)
    i2 = jnp.min(jnp.where(l2 == m2, lane, LANES), axis=-1, keepdims=True)
    e21 = jnp.exp(m2 - m1)
    g1 = 1.0 / (1.0 + e21)
    g2 = e21 / (1.0 + e21)

    oh1 = lane == i1
    oh2 = lane == i2
    both = jnp.where(oh1 | oh2, 1.0, 0.0)
    row = lax.broadcasted_iota(jnp.int32, (TR, TR), 0)
    col = lax.broadcasted_iota(jnp.int32, (TR, TR), 1)
    lower = jnp.where(row > col, 1.0, 0.0).astype(BF16)
    prefix = _dot(lower, both.astype(BF16)) + cnt_ref[...]
    r1 = jnp.sum(jnp.where(oh1, prefix, 0.0), axis=-1, keepdims=True)
    r2 = jnp.sum(jnp.where(oh2, prefix, 0.0), axis=-1, keepdims=True)
    cnt_ref[...] = cnt_ref[...] + jnp.sum(both, axis=0, keepdims=True)

    r1i = r1.astype(jnp.int32)
    r2i = r2.astype(jnp.int32)
    meta = jnp.where(lane == 0, i1, jnp.where(lane == 1, i2, jnp.where(lane == 2, r1i, r2i)))
    meta_ref[...] = meta[:, 0:4]
    gate_ref[...] = jnp.where(lane == 0, g1, g2)[:, 0:2]


def _router(h, gn, rw):
    t, d = h.shape
    return pl.pallas_call(
        _router_kernel,
        grid=(t // TR,),
        in_specs=[
            pl.BlockSpec((TR, d), lambda i: (i, 0)),
            pl.BlockSpec((1, d), lambda i: (0, 0)),
            pl.BlockSpec((d, LANES), lambda i: (0, 0)),
        ],
        out_specs=[
            pl.BlockSpec((TR, d), lambda i: (i, 0)),
            pl.BlockSpec((TR, 4), lambda i: (i, 0)),
            pl.BlockSpec((TR, 2), lambda i: (i, 0)),
            pl.BlockSpec((1, LANES), lambda i: (0, 0)),
        ],
        out_shape=[
            jax.ShapeDtypeStruct((t, d), F32),
            jax.ShapeDtypeStruct((t, 4), jnp.int32),
            jax.ShapeDtypeStruct((t, 2), F32),
            jax.ShapeDtypeStruct((1, LANES), F32),
        ],
        compiler_params=pltpu.CompilerParams(
            dimension_semantics=("arbitrary",), vmem_limit_bytes=VMEM_LIMIT),
        name="moe_router",
    )(h, gn, rw)


def _dispatch_kernel(dest_ref, hn_ref, xg_in_hbm, xg_hbm, sem):
    del xg_in_hbm

    def row_copy(t, k):
        return pltpu.make_async_copy(hn_ref.at[pl.ds(t, 1)], xg_hbm.at[pl.ds(dest_ref[k, t], 1)], sem)

    def issue(t, carry):
        for k in range(TOP_K):
            row_copy(t, k).start()
        return carry

    lax.fori_loop(0, TD, issue, 0, unroll=DMA_UNROLL)

    def drain(t, carry):
        for k in range(TOP_K):
            row_copy(t, k).wait()
        return carry

    lax.fori_loop(0, TD, drain, 0, unroll=DMA_UNROLL)


def _dispatch(dest_t, hn, xg_zero):
    t, d = hn.shape
    return pl.pallas_call(
        _dispatch_kernel,
        grid=(t // TD,),
        in_specs=[
            pl.BlockSpec((TOP_K, TD), lambda i: (0, i), memory_space=pltpu.SMEM),
            pl.BlockSpec((TD, d), lambda i: (i, 0)),
            pl.BlockSpec(memory_space=pl.ANY),
        ],
        out_specs=pl.BlockSpec(memory_space=pl.ANY),
        out_shape=jax.ShapeDtypeStruct(xg_zero.shape, F32),
        scratch_shapes=[pltpu.SemaphoreType.DMA],
        input_output_aliases={2: 0},
        compiler_params=pltpu.CompilerParams(
            dimension_semantics=("arbitrary",), has_side_effects=True,
            vmem_limit_bytes=VMEM_LIMIT),
        name="moe_dispatch",
    )(dest_t, hn, xg_zero)


def _expert_kernel(blk_e_ref, nact_ref, x_ref, wg_ref, wu_ref, wd_ref, o_ref):
    del blk_e_ref

    @pl.when(pl.program_id(0) < nact_ref[0])
    def _():
        xb = x_ref[...].astype(BF16)
        acc = jnp.zeros((MOE_BLOCK, D_MODEL), F32)
        for c in range(D_EXPERT // EXP_CHUNK):
            cols = slice(c * EXP_CHUNK, (c + 1) * EXP_CHUNK)
            g = _dot(xb, wg_ref[:, cols])
            u = _dot(xb, wu_ref[:, cols])
            acc = acc + _dot((_silu(g) * u).astype(BF16), wd_ref[cols, :])
        o_ref[...] = acc

    @pl.when(pl.program_id(0) >= nact_ref[0])
    def _():
        o_ref[...] = jnp.zeros_like(o_ref)


def _experts(blk_e, nact, xg, wg, wu, wd):
    rows, d = xg.shape
    nblk = rows // MOE_BLOCK

    def row_map(i, be, na):
        return (jnp.minimum(i, na[0] - 1), 0)

    def w_map(i, be, na):
        return (be[i], 0, 0)

    grid_spec = pltpu.PrefetchScalarGridSpec(
        num_scalar_prefetch=2,
        grid=(nblk,),
        in_specs=[
            pl.BlockSpec((MOE_BLOCK, d), row_map),
            pl.BlockSpec((None, d, D_EXPERT), w_map),
            pl.BlockSpec((None, d, D_EXPERT), w_map),
            pl.BlockSpec((None, D_EXPERT, d), w_map),
        ],
        out_specs=pl.BlockSpec((MOE_BLOCK, d), lambda i, be, na: (i, 0)),
    )
    return pl.pallas_call(
        _expert_kernel,
        grid_spec=grid_spec,
        out_shape=jax.ShapeDtypeStruct((rows, d), F32),
        compiler_params=pltpu.CompilerParams(
            dimension_semantics=("arbitrary",), vmem_limit_bytes=VMEM_LIMIT),
        name="moe_experts",
    )(blk_e, nact, xg, wg, wu, wd)


def _combine_kernel(dest_ref, h_ref, gate_ref, fn_ref, yb_hbm, o_ref, buf_ref, sem):
    def issue(t, carry):
        for k in range(TOP_K):
            pltpu.make_async_copy(yb_hbm.at[pl.ds(dest_ref[k, t], 1)],
                                  buf_ref.at[k, pl.ds(t, 1)], sem).start()
        return carry

    lax.fori_loop(0, TC, issue, 0, unroll=DMA_UNROLL)

    def drain(t, carry):
        for k in range(TOP_K):
            pltpu.make_async_copy(yb_hbm.at[pl.ds(dest_ref[k, t], 1)],
                                  buf_ref.at[k, pl.ds(t, 1)], sem).wait()
        return carry

    lax.fori_loop(0, TC, drain, 0, unroll=DMA_UNROLL)

    gates = gate_ref[...]
    h = h_ref[...] + (gates[:, 0:1] * buf_ref[0] + gates[:, 1:2] * buf_ref[1])
    o_ref[...] = _rms(h, fn_ref[...])


def _combine(dest_t, h, gates, fn, yb):
    t, d = h.shape
    return pl.pallas_call(
        _combine_kernel,
        grid=(t // TC,),
        in_specs=[
            pl.BlockSpec((TOP_K, TC), lambda i: (0, i), memory_space=pltpu.SMEM),
            pl.BlockSpec((TC, d), lambda i: (i, 0)),
            pl.BlockSpec((TC, TOP_K), lambda i: (i, 0)),
            pl.BlockSpec((1, d), lambda i: (0, 0)),
            pl.BlockSpec(memory_space=pl.ANY),
        ],
        out_specs=pl.BlockSpec((TC, d), lambda i: (i, 0)),
        out_shape=jax.ShapeDtypeStruct((t, d), F32),
        scratch_shapes=[pltpu.VMEM((TOP_K, TC, d), F32), pltpu.SemaphoreType.DMA],
        compiler_params=pltpu.CompilerParams(
            dimension_semantics=("arbitrary",), vmem_limit_bytes=VMEM_LIMIT),
        name="moe_combine",
    )(dest_t, h, gates, fn, yb)


def _retention_tables(s):
    half = RET_DK // 2
    pos = jnp.arange(s, dtype=F32)
    freqs = ROPE_BASE ** (-jnp.arange(half, dtype=F32) / half)
    ang = pos[:, None] * freqs[None, :]
    cos, sin = jnp.cos(ang), jnp.sin(ang)
    cosf = jnp.concatenate([cos, cos], axis=-1)
    sins = jnp.concatenate([-sin, sin], axis=-1)
    log_g = jnp.log1p(-jnp.power(2.0, -5.0 - jnp.arange(RET_HEADS, dtype=F32)))
    idx = jnp.arange(RET_L, dtype=F32)
    ck = jnp.arange(RET_L) // CHUNK
    mask = (ck[:, None] >= ck[None, :]).astype(F32)
    dmat = jnp.exp(log_g[:, None, None] * jnp.abs(idx[:, None] - idx[None, :])) * mask[None]
    qdec = jnp.exp(log_g[:, None] * (idx + 1.0))
    kdec = jnp.exp(log_g[:, None] * (RET_L - 1.0 - idx))
    cdec = jnp.exp(log_g * RET_L)
    qdec = jnp.broadcast_to(qdec[:, :, None], (RET_HEADS, RET_L, RET_DK))
    kdec = jnp.broadcast_to(kdec[:, :, None], (RET_HEADS, RET_L, RET_DK))
    cdec = jnp.broadcast_to(cdec[:, None, None], (RET_HEADS, 1, RET_DV))
    return cosf, sins, dmat, qdec, kdec, cdec


def _moe_plan(meta, counts):
    t = meta.shape[0]
    counts = counts.astype(jnp.int32)
    pcounts = (counts + MOE_BLOCK - 1) // MOE_BLOCK * MOE_BLOCK
    pend = jnp.cumsum(pcounts)
    pstart = pend - pcounts
    dest = pstart[meta[:, 0:TOP_K]] + meta[:, TOP_K:2 * TOP_K]
    rows = (t * TOP_K + MOE_BLOCK - 1) // MOE_BLOCK * MOE_BLOCK + N_EXPERTS * MOE_BLOCK
    nblk = rows // MOE_BLOCK
    nact = pend[-1] // MOE_BLOCK
    blk = jnp.minimum(jnp.arange(nblk, dtype=jnp.int32), nact - 1)
    blk_e = jnp.minimum(jnp.searchsorted(pend, blk * MOE_BLOCK, side='right'), N_EXPERTS - 1)
    return dest.T.astype(jnp.int32), blk_e.astype(jnp.int32), nact.reshape(1).astype(jnp.int32), rows


def kernel(x, e_norm_mix, e_w_in, e_ret_gn, e_dw_w, e_dw_b, e_cv_ln_g, e_cv_ln_b, e_w_out, e_norm_ffn, e_ffn_gate, e_ffn_up, e_ffn_down, o_norm_mix, o_w_in, o_sg_ln_g, o_sg_ln_b, o_sg_w, o_sg_b, o_sc_w, o_w_out, o_norm_ffn, o_router, o_exp_gate, o_exp_up, o_exp_down, final_norm):
    bn, s, d = x.shape
    t = bn * s
    row = lambda v: v.reshape(1, -1)
    cosf, sins, dmat, qdec, kdec, cdec = _retention_tables(s)

    h = _l0_mixer(x, row(e_norm_mix[0]), e_w_in[0].astype(BF16), cosf, sins, dmat, qdec, kdec, cdec,
                  row(e_ret_gn[0]), e_dw_w[0], row(e_dw_b[0]), row(e_cv_ln_g[0]), row(e_cv_ln_b[0]),
                  e_w_out[0].astype(BF16))
    h = _ffn(h.reshape(t, d), row(e_norm_ffn[0]), e_ffn_gate[0].astype(BF16),
             e_ffn_up[0].astype(BF16), e_ffn_down[0].astype(BF16))

    sgb = jnp.broadcast_to(o_sg_b[0][:, :, None], (SG_GROUPS, SG_BLOCK, SG_CH // SG_GROUPS))
    h = _l1_mixer(h.reshape(bn, s, d), row(o_norm_mix[0]), o_w_in[0].astype(BF16),
                  row(o_sg_ln_g[0]), row(o_sg_ln_b[0]), o_sg_w[0], sgb, o_sc_w[0],
                  o_w_out[0].astype(BF16)).reshape(t, d)

    rw = jnp.pad(o_router[0], ((0, 0), (0, LANES - N_EXPERTS)))
    hn, meta, gates, counts = _router(h, row(o_norm_ffn[0]), rw)
    dest_t, blk_e, nact, rows = _moe_plan(meta, counts[0, :N_EXPERTS])
    xg = _dispatch(dest_t, hn, jnp.zeros((rows, d), F32))
    yb = _experts(blk_e, nact, xg, o_exp_gate[0].astype(BF16), o_exp_up[0].astype(BF16),
                  o_exp_down[0].astype(BF16))
    out = _combine(dest_t, h, gates, row(final_norm), yb)
    return out.reshape(bn, s, d)
```

```python
import functools

import jax
import jax.numpy as jnp
from jax import lax
from jax.experimental import pallas as pl
from jax.experimental.pallas import tpu as pltpu

D_MODEL = 1024
CHUNK = 64
EPS = 1e-6
ROPE_BASE = 10000.0
RET_HEADS = 4
RET_DK = 128
RET_DV = 128
RET_WIDTH = RET_HEADS * RET_DV
CONV_CH = 512
CONV_K = 31
SG_CH = 512
SG_GROUPS = 4
SG_BLOCK = 128
SC_CH = 512
SC_K = 3
D_FF = 2816
N_EXPERTS = 8
TOP_K = 2
D_EXPERT = 3584
EVEN_IN = 2 * RET_HEADS * RET_DK + 2 * RET_WIDTH + 2 * CONV_CH
ODD_IN = 2 * SG_CH + 3 * SC_CH

LANES = 128
SUBLANES = 8
TM = 512
RET_L = 256
CONV_HALO = 32
CONV_ROWS = 64
SC_HALO = 8
FF_CHUNK = 1408
TR = 512
TD = 1024
DMA_UNROLL = 8
TC = 512
MOE_BLOCK = 256
EXP_CHUNK = 512
VMEM_LIMIT = 60 * 1024 * 1024

BF16 = jnp.bfloat16
F32 = jnp.float32


def _dot(a, b):
    return jnp.dot(a, b, preferred_element_type=F32)


def _sigmoid(x):
    return 1.0 / (1.0 + jnp.exp(-x))


def _silu(x):
    return x * _sigmoid(x)


def _rms(x, g):
    return x * lax.rsqrt(jnp.mean(x * x, axis=-1, keepdims=True) + EPS) * g


def _ln(x, g, b):
    mu = jnp.mean(x, axis=-1, keepdims=True)
    xc = x - mu
    var = jnp.mean(xc * xc, axis=-1, keepdims=True)
    return xc * lax.rsqrt(var + EPS) * g + b


def _l0_mixer_kernel(x_ref, gn_ref, win_ref, cos_ref, sin_ref, dmat_ref, qdec_ref, kdec_ref,
                     cdec_ref, retgn_ref, dww_ref, dwb_ref, lng_ref, lnb_ref, wout_ref,
                     o_ref, state_ref, cbuf_ref, z_ref, ycat_ref):
    @pl.when(pl.program_id(1) == 0)
    def _():
        state_ref[...] = jnp.zeros_like(state_ref)
        cbuf_ref[0:CONV_HALO, :] = jnp.zeros((CONV_HALO, CONV_CH), F32)

    x = x_ref[0]
    hn = _rms(x, gn_ref[...]).astype(BF16)
    z_ref[...] = _dot(hn, win_ref[...])

    nq = RET_HEADS * RET_DK
    scale = RET_DK ** -0.5
    for j in range(TM // RET_L):
        rows = slice(j * RET_L, (j + 1) * RET_L)
        cosf = cos_ref[rows, :]
        sins = sin_ref[rows, :]
        for h in range(RET_HEADS):
            cols = slice(h * RET_DK, (h + 1) * RET_DK)
            q = z_ref[rows, h * RET_DK:(h + 1) * RET_DK]
            k = z_ref[rows, nq + h * RET_DK:nq + (h + 1) * RET_DK]
            v = z_ref[rows, 2 * nq + h * RET_DV:2 * nq + (h + 1) * RET_DV].astype(BF16)
            g = z_ref[rows, 2 * nq + RET_WIDTH + h * RET_DV:2 * nq + RET_WIDTH + (h + 1) * RET_DV]
            q = (q * cosf + pltpu.roll(q, RET_DK // 2, axis=1) * sins) * scale
            k = k * cosf + pltpu.roll(k, RET_DK // 2, axis=1) * sins
            qb = q.astype(BF16)
            kb = k.astype(BF16)
            sc = lax.dot_general(qb, kb, (((1,), (1,)), ((), ())),
                                 preferred_element_type=F32) * dmat_ref[h]
            st = state_ref[h]
            o = _dot(sc.astype(BF16), v) + _dot((q * qdec_ref[h]).astype(BF16), st.astype(BF16))
            kd_t = (k * kdec_ref[h]).T.astype(BF16)
            state_ref[h] = cdec_ref[h] * st + _dot(kd_t, v)
            mu = jnp.mean(o, axis=-1, keepdims=True)
            oc = o - mu
            var = jnp.mean(oc * oc, axis=-1, keepdims=True)
            r = oc * lax.rsqrt(var + EPS) * retgn_ref[:, cols]
            ycat_ref[rows, cols] = (_silu(g) * r).astype(BF16)

    a = z_ref[:, 2 * nq + 2 * RET_WIDTH:2 * nq + 2 * RET_WIDTH + CONV_CH]
    b = z_ref[:, 2 * nq + 2 * RET_WIDTH + CONV_CH:]
    cbuf_ref[CONV_HALO:CONV_HALO + TM, :] = a * _sigmoid(b)
    first = CONV_HALO - (CONV_K - 1)
    win = CONV_ROWS + CONV_HALO
    for r0 in range(0, TM, CONV_ROWS):
        accs = []
        for l in range(CONV_CH // LANES):
            lanes = slice(l * LANES, (l + 1) * LANES)
            big = cbuf_ref[r0:r0 + win, lanes]
            acc = jnp.zeros((CONV_ROWS, LANES), F32)
            for res in range(SUBLANES):
                rolled = big if res == 0 else pltpu.roll(big, win - res, axis=0)
                for a in range(win // SUBLANES):
                    t = SUBLANES * a + res - first
                    if 0 <= t < CONV_K:
                        acc = acc + rolled[SUBLANES * a:SUBLANES * a + CONV_ROWS, :] * dww_ref[t:t + 1, lanes]
            accs.append(acc)
        cv = _ln(jnp.concatenate(accs, axis=-1) + dwb_ref[...], lng_ref[...], lnb_ref[...])
        ycat_ref[r0:r0 + CONV_ROWS, RET_WIDTH:] = _silu(cv).astype(BF16)
    cbuf_ref[0:CONV_HALO, :] = cbuf_ref[TM:TM + CONV_HALO, :]

    o_ref[0] = x + _dot(ycat_ref[...], wout_ref[...])


def _l0_mixer(x, gn, win, cosf, sins, dmat, qdec, kdec, cdec, retgn, dww, dwb, lng, lnb, wout):
    bn, s, d = x.shape
    full = lambda shape: pl.BlockSpec(shape, lambda b, i: (0,) * len(shape))
    return pl.pallas_call(
        _l0_mixer_kernel,
        grid=(bn, s // TM),
        in_specs=[
            pl.BlockSpec((1, TM, d), lambda b, i: (b, i, 0)),
            full((1, d)),
            full((d, EVEN_IN)),
            pl.BlockSpec((TM, RET_DK), lambda b, i: (i, 0)),
            pl.BlockSpec((TM, RET_DK), lambda b, i: (i, 0)),
            full((RET_HEADS, RET_L, RET_L)),
            full((RET_HEADS, RET_L, RET_DK)),
            full((RET_HEADS, RET_L, RET_DK)),
            full((RET_HEADS, 1, RET_DV)),
            full((1, RET_WIDTH)),
            full((CONV_K, CONV_CH)),
            full((1, CONV_CH)),
            full((1, CONV_CH)),
            full((1, CONV_CH)),
            full((RET_WIDTH + CONV_CH, d)),
        ],
        out_specs=pl.BlockSpec((1, TM, d), lambda b, i: (b, i, 0)),
        out_shape=jax.ShapeDtypeStruct((bn, s, d), F32),
        scratch_shapes=[
            pltpu.VMEM((RET_HEADS, RET_DK, RET_DV), F32),
            pltpu.VMEM((TM + CONV_HALO, CONV_CH), F32),
            pltpu.VMEM((TM, EVEN_IN), F32),
            pltpu.VMEM((TM, RET_WIDTH + CONV_CH), BF16),
        ],
        compiler_params=pltpu.CompilerParams(
            dimension_semantics=("arbitrary", "arbitrary"), vmem_limit_bytes=VMEM_LIMIT),
        name="l0_mixer",
    )(x, gn, win, cosf, sins, dmat, qdec, kdec, cdec, retgn, dww, dwb, lng, lnb, wout)


def _ffn_kernel(h_ref, gn_ref, wg_ref, wu_ref, wd_ref, o_ref):
    h = h_ref[...]
    hn = _rms(h, gn_ref[...]).astype(BF16)
    acc = h
    for c in range(D_FF // FF_CHUNK):
        cols = slice(c * FF_CHUNK, (c + 1) * FF_CHUNK)
        g = _dot(hn, wg_ref[:, cols])
        u = _dot(hn, wu_ref[:, cols])
        acc = acc + _dot((_silu(g) * u).astype(BF16), wd_ref[cols, :])
    o_ref[...] = acc


def _ffn(h, gn, wg, wu, wd):
    t, d = h.shape
    full = lambda shape: pl.BlockSpec(shape, lambda i: (0,) * len(shape))
    return pl.pallas_call(
        _ffn_kernel,
        grid=(t // TM,),
        in_specs=[
            pl.BlockSpec((TM, d), lambda i: (i, 0)),
            full((1, d)),
            full((d, D_FF)),
            full((d, D_FF)),
            full((D_FF, d)),
        ],
        out_specs=pl.BlockSpec((TM, d), lambda i: (i, 0)),
        out_shape=jax.ShapeDtypeStruct((t, d), F32),
        compiler_params=pltpu.CompilerParams(
            dimension_semantics=("arbitrary",), vmem_limit_bytes=VMEM_LIMIT),
        name="ffn_swiglu",
    )(h, gn, wg, wu, wd)


def _l1_mixer_kernel(x_ref, gn_ref, win_ref, lng_ref, lnb_ref, sgw_ref, sgb_ref, scw_ref, wout_ref,
                     o_ref, pbuf_ref, z_ref, ycat_ref):
    @pl.when(pl.program_id(1) == 0)
    def _():
        pbuf_ref[0:SC_HALO, :] = jnp.zeros((SC_HALO, SC_CH), F32)

    x = x_ref[0]
    hn = _rms(x, gn_ref[...]).astype(BF16)
    z_ref[...] = _dot(hn, win_ref[...])

    vn = _ln(jax.nn.gelu(z_ref[:, SG_CH:2 * SG_CH]), lng_ref[...], lnb_ref[...]).astype(BF16)
    gw = SG_CH // SG_GROUPS
    pc = lax.broadcasted_iota(jnp.int32, (SG_BLOCK, SG_BLOCK), 0) // CHUNK
    qc = lax.broadcasted_iota(jnp.int32, (SG_BLOCK, SG_BLOCK), 1) // CHUNK
    for g in range(SG_GROUPS):
        cols = slice(g * gw, (g + 1) * gw)
        ws = jnp.where(pc >= qc, sgw_ref[g], 0.0).astype(BF16)
        for m in range(TM // SG_BLOCK):
            rows = slice(m * SG_BLOCK, (m + 1) * SG_BLOCK)
            sp = _dot(ws, vn[rows, cols]) + sgb_ref[g]
            u = jax.nn.gelu(z_ref[rows, g * gw:(g + 1) * gw])
            ycat_ref[rows, cols] = (u * sp).astype(BF16)

    base = 2 * SG_CH
    bg = z_ref[:, base:base + SC_CH]
    cg = z_ref[:, base + SC_CH:base + 2 * SC_CH]
    hv = z_ref[:, base + 2 * SC_CH:]
    pbuf_ref[SC_HALO:SC_HALO + TM, :] = cg * hv
    first = SC_HALO - (SC_K - 1)
    conv = jnp.zeros((TM, SC_CH), F32)
    for t in range(SC_K):
        conv = conv + pbuf_ref[first + t:first + t + TM, :] * scw_ref[t:t + 1, :]
    ycat_ref[:, SG_CH:] = (bg * conv).astype(BF16)
    pbuf_ref[0:SC_HALO, :] = pbuf_ref[TM:TM + SC_HALO, :]

    o_ref[0] = x + _dot(ycat_ref[...], wout_ref[...])


def _l1_mixer(x, gn, win, lng, lnb, sgw, sgb, scw, wout):
    bn, s, d = x.shape
    full = lambda shape: pl.BlockSpec(shape, lambda b, i: (0,) * len(shape))
    return pl.pallas_call(
        _l1_mixer_kernel,
        grid=(bn, s // TM),
        in_specs=[
            pl.BlockSpec((1, TM, d), lambda b, i: (b, i, 0)),
            full((1, d)),
            full((d, ODD_IN)),
            full((1, SG_CH)),
            full((1, SG_CH)),
            full((SG_GROUPS, SG_BLOCK, SG_BLOCK)),
            full((SG_GROUPS, SG_BLOCK, SG_CH // SG_GROUPS)),
            full((SC_K, SC_CH)),
            full((SG_CH + SC_CH, d)),
        ],
        out_specs=pl.BlockSpec((1, TM, d), lambda b, i: (b, i, 0)),
        out_shape=jax.ShapeDtypeStruct((bn, s, d), F32),
        scratch_shapes=[
            pltpu.VMEM((TM + SC_HALO, SC_CH), F32),
            pltpu.VMEM((TM, ODD_IN), F32),
            pltpu.VMEM((TM, SG_CH + SC_CH), BF16),
        ],
        compiler_params=pltpu.CompilerParams(
            dimension_semantics=("arbitrary", "arbitrary"), vmem_limit_bytes=VMEM_LIMIT),
        name="l1_mixer",
    )(x, gn, win, lng, lnb, sgw, sgb, scw, wout)


def _router_kernel(h_ref, gn_ref, rw_ref, hn_ref, meta_ref, gate_ref, cnt_ref):
    @pl.when(pl.program_id(0) == 0)
    def _():
        cnt_ref[...] = jnp.zeros_like(cnt_ref)

    hn = _rms(h_ref[...], gn_ref[...])
    hn_ref[...] = hn
    hn_hi = hn.astype(BF16)
    hn_lo = (hn - hn_hi.astype(F32)).astype(BF16)
    rw = rw_ref[...]
    rw_hi = rw.astype(BF16)
    rw_lo = (rw - rw_hi.astype(F32)).astype(BF16)
    logits = _dot(hn_hi, rw_hi) + (_dot(hn_hi, rw_lo) + _dot(hn_lo, rw_hi))

    lane = lax.broadcasted_iota(jnp.int32, (TR, LANES), 1)
    neg = jnp.float32(-jnp.inf)
    l1 = jnp.where(lane < N_EXPERTS, logits, neg)
    m1 = jnp.max(l1, axis=-1, keepdims=True)
    i1 = jnp.min(jnp.where(l1 == m1, lane, LANES), axis=-1, keepdims=True)
    l2 = jnp.where(lane == i1, neg, l1)
    m2 = jnp.max(l2, axis=-1, keepdims=True)
    i2 = jnp.min(jnp.where(l2 == m2, lane, LANES), axis=-1, keepdims=True)
    e21 = jnp.exp(m2 - m1)
    g1 = 1.0 / (1.0 + e21)
    g2 = e21 / (1.0 + e21)

    oh1 = lane == i1
    oh2 = lane == i2
    both = jnp.where(oh1 | oh2, 1.0, 0.0)
    row = lax.broadcasted_iota(jnp.int32, (TR, TR), 0)
    col = lax.broadcasted_iota(jnp.int32, (TR, TR), 1)
    lower = jnp.where(row > col, 1.0, 0.0).astype(BF16)
    prefix = _dot(lower, both.astype(BF16)) + cnt_ref[...]
    r1 = jnp.sum(jnp.where(oh1, prefix, 0.0), axis=-1, keepdims=True)
    r2 = jnp.sum(jnp.where(oh2, prefix, 0.0), axis=-1, keepdims=True)
    cnt_ref[...] = cnt_ref[...] + jnp.sum(both, axis=0, keepdims=True)

    r1i = r1.astype(jnp.int32)
    r2i = r2.astype(jnp.int32)
    meta = jnp.where(lane == 0, i1, jnp.where(lane == 1, i2, jnp.where(lane == 2, r1i, r2i)))
    meta_ref[...] = meta[:, 0:4]
    gate_ref[...] = jnp.where(lane == 0, g1, g2)[:, 0:2]


def _router(h, gn, rw):
    t, d = h.shape
    return pl.pallas_call(
        _router_kernel,
        grid=(t // TR,),
        in_specs=[
            pl.BlockSpec((TR, d), lambda i: (i, 0)),
            pl.BlockSpec((1, d), lambda i: (0, 0)),
            pl.BlockSpec((d, LANES), lambda i: (0, 0)),
        ],
        out_specs=[
            pl.BlockSpec((TR, d), lambda i: (i, 0)),
            pl.BlockSpec((TR, 4), lambda i: (i, 0)),
            pl.BlockSpec((TR, 2), lambda i: (i, 0)),
            pl.BlockSpec((1, LANES), lambda i: (0, 0)),
        ],
        out_shape=[
            jax.ShapeDtypeStruct((t, d), F32),
            jax.ShapeDtypeStruct((t, 4), jnp.int32),
            jax.ShapeDtypeStruct((t, 2), F32),
            jax.ShapeDtypeStruct((1, LANES), F32),
        ],
        compiler_params=pltpu.CompilerParams(
            dimension_semantics=("arbitrary",), vmem_limit_bytes=VMEM_LIMIT),
        name="moe_router",
    )(h, gn, rw)


def _dispatch_kernel(dest_ref, hn_ref, xg_in_hbm, xg_hbm, sem):
    del xg_in_hbm

    def row_copy(t, k):
        return pltpu.make_async_copy(hn_ref.at[pl.ds(t, 1)], xg_hbm.at[pl.ds(dest_ref[k, t], 1)], sem)

    def issue(t, carry):
        for k in range(TOP_K):
            row_copy(t, k).start()
        return carry

    lax.fori_loop(0, TD, issue, 0, unroll=DMA_UNROLL)

    def drain(t, carry):
        for k in range(TOP_K):
            row_copy(t, k).wait()
        return carry

    lax.fori_loop(0, TD, drain, 0, unroll=DMA_UNROLL)


def _dispatch(dest_t, hn, xg_zero):
    t, d = hn.shape
    return pl.pallas_call(
        _dispatch_kernel,
        grid=(t // TD,),
        in_specs=[
            pl.BlockSpec((TOP_K, TD), lambda i: (0, i), memory_space=pltpu.SMEM),
            pl.BlockSpec((TD, d), lambda i: (i, 0)),
            pl.BlockSpec(memory_space=pl.ANY),
        ],
        out_specs=pl.BlockSpec(memory_space=pl.ANY),
        out_shape=jax.ShapeDtypeStruct(xg_zero.shape, F32),
        scratch_shapes=[pltpu.SemaphoreType.DMA],
        input_output_aliases={2: 0},
        compiler_params=pltpu.CompilerParams(
            dimension_semantics=("arbitrary",), has_side_effects=True,
            vmem_limit_bytes=VMEM_LIMIT),
        name="moe_dispatch",
    )(dest_t, hn, xg_zero)


def _expert_kernel(blk_e_ref, nact_ref, x_ref, wg_ref, wu_ref, wd_ref, o_ref):
    del blk_e_ref

    @pl.when(pl.program_id(0) < nact_ref[0])
    def _():
        xb = x_ref[...].astype(BF16)
        acc = jnp.zeros((MOE_BLOCK, D_MODEL), F32)
        for c in range(D_EXPERT // EXP_CHUNK):
            cols = slice(c * EXP_CHUNK, (c + 1) * EXP_CHUNK)
            g = _dot(xb, wg_ref[:, cols])
            u = _dot(xb, wu_ref[:, cols])
            acc = acc + _dot((_silu(g) * u).astype(BF16), wd_ref[cols, :])
        o_ref[...] = acc

    @pl.when(pl.program_id(0) >= nact_ref[0])
    def _():
        o_ref[...] = jnp.zeros_like(o_ref)


def _experts(blk_e, nact, xg, wg, wu, wd):
    rows, d = xg.shape
    nblk = rows // MOE_BLOCK

    def row_map(i, be, na):
        return (jnp.minimum(i, na[0] - 1), 0)

    def w_map(i, be, na):
        return (be[i], 0, 0)

    grid_spec = pltpu.PrefetchScalarGridSpec(
        num_scalar_prefetch=2,
        grid=(nblk,),
        in_specs=[
            pl.BlockSpec((MOE_BLOCK, d), row_map),
            pl.BlockSpec((None, d, D_EXPERT), w_map),
            pl.BlockSpec((None, d, D_EXPERT), w_map),
            pl.BlockSpec((None, D_EXPERT, d), w_map),
        ],
        out_specs=pl.BlockSpec((MOE_BLOCK, d), lambda i, be, na: (i, 0)),
    )
    return pl.pallas_call(
        _expert_kernel,
        grid_spec=grid_spec,
        out_shape=jax.ShapeDtypeStruct((rows, d), F32),
        compiler_params=pltpu.CompilerParams(
            dimension_semantics=("arbitrary",), vmem_limit_bytes=VMEM_LIMIT),
        name="moe_experts",
    )(blk_e, nact, xg, wg, wu, wd)


def _combine_kernel(dest_ref, h_ref, gate_ref, fn_ref, yb_hbm, o_ref, buf_ref, sem):
    def issue(t, carry):
        for k in range(TOP_K):
            pltpu.make_async_copy(yb_hbm.at[pl.ds(dest_ref[k, t], 1)],
                                  buf_ref.at[k, pl.ds(t, 1)], sem).start()
        return carry

    lax.fori_loop(0, TC, issue, 0, unroll=DMA_UNROLL)

    def drain(t, carry):
        for k in range(TOP_K):
            pltpu.make_async_copy(yb_hbm.at[pl.ds(dest_ref[k, t], 1)],
                                  buf_ref.at[k, pl.ds(t, 1)], sem).wait()
        return carry

    lax.fori_loop(0, TC, drain, 0, unroll=DMA_UNROLL)

    gates = gate_ref[...]
    h = h_ref[...] + (gates[:, 0:1] * buf_ref[0] + gates[:, 1:2] * buf_ref[1])
    o_ref[...] = _rms(h, fn_ref[...])


def _combine(dest_t, h, gates, fn, yb):
    t, d = h.shape
    return pl.pallas_call(
        _combine_kernel,
        grid=(t // TC,),
        in_specs=[
            pl.BlockSpec((TOP_K, TC), lambda i: (0, i), memory_space=pltpu.SMEM),
            pl.BlockSpec((TC, d), lambda i: (i, 0)),
            pl.BlockSpec((TC, TOP_K), lambda i: (i, 0)),
            pl.BlockSpec((1, d), lambda i: (0, 0)),
            pl.BlockSpec(memory_space=pl.ANY),
        ],
        out_specs=pl.BlockSpec((TC, d), lambda i: (i, 0)),
        out_shape=jax.ShapeDtypeStruct((t, d), F32),
        scratch_shapes=[pltpu.VMEM((TOP_K, TC, d), F32), pltpu.SemaphoreType.DMA],
        compiler_params=pltpu.CompilerParams(
            dimension_semantics=("arbitrary",), vmem_limit_bytes=VMEM_LIMIT),
        name="moe_combine",
    )(dest_t, h, gates, fn, yb)


def _retention_tables(s):
    half = RET_DK // 2
    pos = jnp.arange(s, dtype=F32)
    freqs = ROPE_BASE ** (-jnp.arange(half, dtype=F32) / half)
    ang = pos[:, None] * freqs[None, :]
    cos, sin = jnp.cos(ang), jnp.sin(ang)
    cosf = jnp.concatenate([cos, cos], axis=-1)
    sins = jnp.concatenate([-sin, sin], axis=-1)
    log_g = jnp.log1p(-jnp.power(2.0, -5.0 - jnp.arange(RET_HEADS, dtype=F32)))
    idx = jnp.arange(RET_L, dtype=F32)
    ck = jnp.arange(RET_L) // CHUNK
    mask = (ck[:, None] >= ck[None, :]).astype(F32)
    dmat = jnp.exp(log_g[:, None, None] * jnp.abs(idx[:, None] - idx[None, :])) * mask[None]
    qdec = jnp.exp(log_g[:, None] * (idx + 1.0))
    kdec = jnp.exp(log_g[:, None] * (RET_L - 1.0 - idx))
    cdec = jnp.exp(log_g * RET_L)
    qdec = jnp.broadcast_to(qdec[:, :, None], (RET_HEADS, RET_L, RET_DK))
    kdec = jnp.broadcast_to(kdec[:, :, None], (RET_HEADS, RET_L, RET_DK))
    cdec = jnp.broadcast_to(cdec[:, None, None], (RET_HEADS, 1, RET_DV))
    return cosf, sins, dmat, qdec, kdec, cdec


def _moe_plan(meta, counts):
    t = meta.shape[0]
    counts = counts.astype(jnp.int32)
    pcounts = (counts + MOE_BLOCK - 1) // MOE_BLOCK * MOE_BLOCK
    pend = jnp.cumsum(pcounts)
    pstart = pend - pcounts
    dest = pstart[meta[:, 0:TOP_K]] + meta[:, TOP_K:2 * TOP_K]
    rows = (t * TOP_K + MOE_BLOCK - 1) // MOE_BLOCK * MOE_BLOCK + N_EXPERTS * MOE_BLOCK
    nblk = rows // MOE_BLOCK
    nact = pend[-1] // MOE_BLOCK
    blk = jnp.minimum(jnp.arange(nblk, dtype=jnp.int32), nact - 1)
    blk_e = jnp.minimum(jnp.searchsorted(pend, blk * MOE_BLOCK, side='right'), N_EXPERTS - 1)
    return dest.T.astype(jnp.int32), blk_e.astype(jnp.int32), nact.reshape(1).astype(jnp.int32), rows


def kernel(x, e_norm_mix, e_w_in, e_ret_gn, e_dw_w, e_dw_b, e_cv_ln_g, e_cv_ln_b, e_w_out, e_norm_ffn, e_ffn_gate, e_ffn_up, e_ffn_down, o_norm_mix, o_w_in, o_sg_ln_g, o_sg_ln_b, o_sg_w, o_sg_b, o_sc_w, o_w_out, o_norm_ffn, o_router, o_exp_gate, o_exp_up, o_exp_down, final_norm):
    bn, s, d = x.shape
    t = bn * s
    row = lambda v: v.reshape(1, -1)
    cosf, sins, dmat, qdec, kdec, cdec = _retention_tables(s)

    h = _l0_mixer(x, row(e_norm_mix[0]), e_w_in[0].astype(BF16), cosf, sins, dmat, qdec, kdec, cdec,
                  row(e_ret_gn[0]), e_dw_w[0], row(e_dw_b[0]), row(e_cv_ln_g[0]), row(e_cv_ln_b[0]),
                  e_w_out[0].astype(BF16))
    h = _ffn(h.reshape(t, d), row(e_norm_ffn[0]), e_ffn_gate[0].astype(BF16),
             e_ffn_up[0].astype(BF16), e_ffn_down[0].astype(BF16))

    sgb = jnp.broadcast_to(o_sg_b[0][:, :, None], (SG_GROUPS, SG_BLOCK, SG_CH // SG_GROUPS))
    h = _l1_mixer(h.reshape(bn, s, d), row(o_norm_mix[0]), o_w_in[0].astype(BF16),
                  row(o_sg_ln_g[0]), row(o_sg_ln_b[0]), o_sg_w[0], sgb, o_sc_w[0],
                  o_w_out[0].astype(BF16)).reshape(t, d)

    rw = jnp.pad(o_router[0], ((0, 0), (0, LANES - N_EXPERTS)))
    hn, meta, gates, counts = _router(h, row(o_norm_ffn[0]), rw)
    dest_t, blk_e, nact, rows = _moe_plan(meta, counts[0, :N_EXPERTS])
    xg = _dispatch(dest_t, hn, jnp.zeros((rows, d), F32))
    yb = _experts(blk_e, nact, xg, o_exp_gate[0].astype(BF16), o_exp_up[0].astype(BF16),
                  o_exp_down[0].astype(BF16))
    out = _combine(dest_t, h, gates, row(final_norm), yb)
    return out.reshape(bn, s, d)
```

```python
import functools

import jax
import jax.numpy as jnp
from jax import lax
from jax.experimental import pallas as pl
from jax.experimental.pallas import tpu as pltpu

D_MODEL = 1024
CHUNK = 64
EPS = 1e-6
ROPE_BASE = 10000.0
RET_HEADS = 4
RET_DK = 128
RET_DV = 128
RET_WIDTH = RET_HEADS * RET_DV
CONV_CH = 512
CONV_K = 31
SG_CH = 512
SG_GROUPS = 4
SG_BLOCK = 128
SC_CH = 512
SC_K = 3
D_FF = 2816
N_EXPERTS = 8
TOP_K = 2
D_EXPERT = 3584
EVEN_IN = 2 * RET_HEADS * RET_DK + 2 * RET_WIDTH + 2 * CONV_CH
ODD_IN = 2 * SG_CH + 3 * SC_CH

LANES = 128
SUBLANES = 8
TM = 512
RET_L = 256
CONV_HALO = 32
CONV_ROWS = 64
SC_HALO = 8
FF_CHUNK = 1408
TR = 512
TD = 1024
DMA_UNROLL = 8
TC = 512
MOE_BLOCK = 256
N_ZERO_BLOCKS = 2 * N_EXPERTS
EXP_CHUNK = 512
VMEM_LIMIT = 60 * 1024 * 1024

BF16 = jnp.bfloat16
F32 = jnp.float32


def _dot(a, b):
    return jnp.dot(a, b, preferred_element_type=F32)


def _sigmoid(x):
    return 1.0 / (1.0 + jnp.exp(-x))


def _silu(x):
    return x * _sigmoid(x)


def _rms(x, g):
    return x * lax.rsqrt(jnp.mean(x * x, axis=-1, keepdims=True) + EPS) * g


def _ln(x, g, b):
    mu = jnp.mean(x, axis=-1, keepdims=True)
    xc = x - mu
    var = jnp.mean(xc * xc, axis=-1, keepdims=True)
    return xc * lax.rsqrt(var + EPS) * g + b


def _l0_mixer_kernel(x_ref, gn_ref, win_ref, cos_ref, sin_ref, dmat_ref, qdec_ref, kdec_ref,
                     cdec_ref, retgn_ref, dww_ref, dwb_ref, lng_ref, lnb_ref, wout_ref,
                     o_ref, state_ref, cbuf_ref, z_ref, ycat_ref):
    @pl.when(pl.program_id(1) == 0)
    def _():
        state_ref[...] = jnp.zeros_like(state_ref)
        cbuf_ref[0:CONV_HALO, :] = jnp.zeros((CONV_HALO, CONV_CH), F32)

    x = x_ref[0]
    hn = _rms(x, gn_ref[...]).astype(BF16)
    z_ref[...] = _dot(hn, win_ref[...])

    nq = RET_HEADS * RET_DK
    scale = RET_DK ** -0.5
    for j in range(TM // RET_L):
        rows = slice(j * RET_L, (j + 1) * RET_L)
        cosf = cos_ref[rows, :]
        sins = sin_ref[rows, :]
        for h in range(RET_HEADS):
            cols = slice(h * RET_DK, (h + 1) * RET_DK)
            q = z_ref[rows, h * RET_DK:(h + 1) * RET_DK]
            k = z_ref[rows, nq + h * RET_DK:nq + (h + 1) * RET_DK]
            v = z_ref[rows, 2 * nq + h * RET_DV:2 * nq + (h + 1) * RET_DV].astype(BF16)
            g = z_ref[rows, 2 * nq + RET_WIDTH + h * RET_DV:2 * nq + RET_WIDTH + (h + 1) * RET_DV]
            q = (q * cosf + pltpu.roll(q, RET_DK // 2, axis=1) * sins) * scale
            k = k * cosf + pltpu.roll(k, RET_DK // 2, axis=1) * sins
            qb = q.astype(BF16)
            kb = k.astype(BF16)
            sc = lax.dot_general(qb, kb, (((1,), (1,)), ((), ())),
                                 preferred_element_type=F32) * dmat_ref[h]
            st = state_ref[h]
            o = _dot(sc.astype(BF16), v) + _dot((q * qdec_ref[h]).astype(BF16), st.astype(BF16))
            kd_t = (k * kdec_ref[h]).T.astype(BF16)
            state_ref[h] = cdec_ref[h] * st + _dot(kd_t, v)
            mu = jnp.mean(o, axis=-1, keepdims=True)
            oc = o - mu
            var = jnp.mean(oc * oc, axis=-1, keepdims=True)
            r = oc * lax.rsqrt(var + EPS) * retgn_ref[:, cols]
            ycat_ref[rows, cols] = (_silu(g) * r).astype(BF16)

    a = z_ref[:, 2 * nq + 2 * RET_WIDTH:2 * nq + 2 * RET_WIDTH + CONV_CH]
    b = z_ref[:, 2 * nq + 2 * RET_WIDTH + CONV_CH:]
    cbuf_ref[CONV_HALO:CONV_HALO + TM, :] = a * _sigmoid(b)
    first = CONV_HALO - (CONV_K - 1)
    win = CONV_ROWS + CONV_HALO
    for r0 in range(0, TM, CONV_ROWS):
        accs = []
        for l in range(CONV_CH // LANES):
            lanes = slice(l * LANES, (l + 1) * LANES)
            big = cbuf_ref[r0:r0 + win, lanes]
            acc = jnp.zeros((CONV_ROWS, LANES), F32)
            for res in range(SUBLANES):
                rolled = big if res == 0 else pltpu.roll(big, win - res, axis=0)
                for a in range(win // SUBLANES):
                    t = SUBLANES * a + res - first
                    if 0 <= t < CONV_K:
                        acc = acc + rolled[SUBLANES * a:SUBLANES * a + CONV_ROWS, :] * dww_ref[t:t + 1, lanes]
            accs.append(acc)
        cv = _ln(jnp.concatenate(accs, axis=-1) + dwb_ref[...], lng_ref[...], lnb_ref[...])
        ycat_ref[r0:r0 + CONV_ROWS, RET_WIDTH:] = _silu(cv).astype(BF16)
    cbuf_ref[0:CONV_HALO, :] = cbuf_ref[TM:TM + CONV_HALO, :]

    o_ref[0] = x + _dot(ycat_ref[...], wout_ref[...])


def _l0_mixer(x, gn, win, cosf, sins, dmat, qdec, kdec, cdec, retgn, dww, dwb, lng, lnb, wout):
    bn, s, d = x.shape
    full = lambda shape: pl.BlockSpec(shape, lambda b, i: (0,) * len(shape))
    return pl.pallas_call(
        _l0_mixer_kernel,
        grid=(bn, s // TM),
        in_specs=[
            pl.BlockSpec((1, TM, d), lambda b, i: (b, i, 0)),
            full((1, d)),
            full((d, EVEN_IN)),
            pl.BlockSpec((TM, RET_DK), lambda b, i: (i, 0)),
            pl.BlockSpec((TM, RET_DK), lambda b, i: (i, 0)),
            full((RET_HEADS, RET_L, RET_L)),
            full((RET_HEADS, RET_L, RET_DK)),
            full((RET_HEADS, RET_L, RET_DK)),
            full((RET_HEADS, 1, RET_DV)),
            full((1, RET_WIDTH)),
            full((CONV_K, CONV_CH)),
            full((1, CONV_CH)),
            full((1, CONV_CH)),
            full((1, CONV_CH)),
            full((RET_WIDTH + CONV_CH, d)),
        ],
        out_specs=pl.BlockSpec((1, TM, d), lambda b, i: (b, i, 0)),
        out_shape=jax.ShapeDtypeStruct((bn, s, d), F32),
        scratch_shapes=[
            pltpu.VMEM((RET_HEADS, RET_DK, RET_DV), F32),
            pltpu.VMEM((TM + CONV_HALO, CONV_CH), F32),
            pltpu.VMEM((TM, EVEN_IN), F32),
            pltpu.VMEM((TM, RET_WIDTH + CONV_CH), BF16),
        ],
        compiler_params=pltpu.CompilerParams(
            dimension_semantics=("arbitrary", "arbitrary"), vmem_limit_bytes=VMEM_LIMIT),
        name="l0_mixer",
    )(x, gn, win, cosf, sins, dmat, qdec, kdec, cdec, retgn, dww, dwb, lng, lnb, wout)


def _ffn_kernel(h_ref, gn_ref, wg_ref, wu_ref, wd_ref, o_ref):
    h = h_ref[...]
    hn = _rms(h, gn_ref[...]).astype(BF16)
    acc = h
    for c in range(D_FF // FF_CHUNK):
        cols = slice(c * FF_CHUNK, (c + 1) * FF_CHUNK)
        g = _dot(hn, wg_ref[:, cols])
        u = _dot(hn, wu_ref[:, cols])
        acc = acc + _dot((_silu(g) * u).astype(BF16), wd_ref[cols, :])
    o_ref[...] = acc


def _ffn(h, gn, wg, wu, wd):
    t, d = h.shape
    full = lambda shape: pl.BlockSpec(shape, lambda i: (0,) * len(shape))
    return pl.pallas_call(
        _ffn_kernel,
        grid=(t // TM,),
        in_specs=[
            pl.BlockSpec((TM, d), lambda i: (i, 0)),
            full((1, d)),
            full((d, D_FF)),
            full((d, D_FF)),
            full((D_FF, d)),
        ],
        out_specs=pl.BlockSpec((TM, d), lambda i: (i, 0)),
        out_shape=jax.ShapeDtypeStruct((t, d), F32),
        compiler_params=pltpu.CompilerParams(
            dimension_semantics=("arbitrary",), vmem_limit_bytes=VMEM_LIMIT),
        name="ffn_swiglu",
    )(h, gn, wg, wu, wd)


def _l1_mixer_kernel(x_ref, gn_ref, win_ref, lng_ref, lnb_ref, sgw_ref, sgb_ref, scw_ref, wout_ref,
                     o_ref, pbuf_ref, z_ref, ycat_ref):
    @pl.when(pl.program_id(1) == 0)
    def _():
        pbuf_ref[0:SC_HALO, :] = jnp.zeros((SC_HALO, SC_CH), F32)

    x = x_ref[0]
    hn = _rms(x, gn_ref[...]).astype(BF16)
    z_ref[...] = _dot(hn, win_ref[...])

    vn = _ln(jax.nn.gelu(z_ref[:, SG_CH:2 * SG_CH]), lng_ref[...], lnb_ref[...]).astype(BF16)
    gw = SG_CH // SG_GROUPS
    pc = lax.broadcasted_iota(jnp.int32, (SG_BLOCK, SG_BLOCK), 0) // CHUNK
    qc = lax.broadcasted_iota(jnp.int32, (SG_BLOCK, SG_BLOCK), 1) // CHUNK
    for g in range(SG_GROUPS):
        cols = slice(g * gw, (g + 1) * gw)
        ws = jnp.where(pc >= qc, sgw_ref[g], 0.0).astype(BF16)
        for m in range(TM // SG_BLOCK):
            rows = slice(m * SG_BLOCK, (m + 1) * SG_BLOCK)
            sp = _dot(ws, vn[rows, cols]) + sgb_ref[g]
            u = jax.nn.gelu(z_ref[rows, g * gw:(g + 1) * gw])
            ycat_ref[rows, cols] = (u * sp).astype(BF16)

    base = 2 * SG_CH
    bg = z_ref[:, base:base + SC_CH]
    cg = z_ref[:, base + SC_CH:base + 2 * SC_CH]
    hv = z_ref[:, base + 2 * SC_CH:]
    pbuf_ref[SC_HALO:SC_HALO + TM, :] = cg * hv
    first = SC_HALO - (SC_K - 1)
    conv = jnp.zeros((TM, SC_CH), F32)
    for t in range(SC_K):
        conv = conv + pbuf_ref[first + t:first + t + TM, :] * scw_ref[t:t + 1, :]
    ycat_ref[:, SG_CH:] = (bg * conv).astype(BF16)
    pbuf_ref[0:SC_HALO, :] = pbuf_ref[TM:TM + SC_HALO, :]

    o_ref[0] = x + _dot(ycat_ref[...], wout_ref[...])


def _l1_mixer(x, gn, win, lng, lnb, sgw, sgb, scw, wout):
    bn, s, d = x.shape
    full = lambda shape: pl.BlockSpec(shape, lambda b, i: (0,) * len(shape))
    return pl.pallas_call(
        _l1_mixer_kernel,
        grid=(bn, s // TM),
        in_specs=[
            pl.BlockSpec((1, TM, d), lambda b, i: (b, i, 0)),
            full((1, d)),
            full((d, ODD_IN)),
            full((1, SG_CH)),
            full((1, SG_CH)),
            full((SG_GROUPS, SG_BLOCK, SG_BLOCK)),
            full((SG_GROUPS, SG_BLOCK, SG_CH // SG_GROUPS)),
            full((SC_K, SC_CH)),
            full((SG_CH + SC_CH, d)),
        ],
        out_specs=pl.BlockSpec((1, TM, d), lambda b, i: (b, i, 0)),
        out_shape=jax.ShapeDtypeStruct((bn, s, d), F32),
        scratch_shapes=[
            pltpu.VMEM((TM + SC_HALO, SC_CH), F32),
            pltpu.VMEM((TM, ODD_IN), F32),
            pltpu.VMEM((TM, SG_CH + SC_CH), BF16),
        ],
        compiler_params=pltpu.CompilerParams(
            dimension_semantics=("arbitrary", "arbitrary"), vmem_limit_bytes=VMEM_LIMIT),
        name="l1_mixer",
    )(x, gn, win, lng, lnb, sgw, sgb, scw, wout)


def _router_kernel(h_ref, gn_ref, rw_ref, hn_ref, meta_ref, gate_ref, cnt_ref):
    @pl.when(pl.program_id(0) == 0)
    def _():
        cnt_ref[...] = jnp.zeros_like(cnt_ref)

    hn = _rms(h_ref[...], gn_ref[...])
    hn_ref[...] = hn
    hn_hi = hn.astype(BF16)
    hn_lo = (hn - hn_hi.astype(F32)).astype(BF16)
    rw = rw_ref[...]
    rw_hi = rw.astype(BF16)
    rw_lo = (rw - rw_hi.astype(F32)).astype(BF16)
    logits = _dot(hn_hi, rw_hi) + (_dot(hn_hi, rw_lo) + _dot(hn_lo, rw_hi))

    lane = lax.broadcasted_iota(jnp.int32, (TR, LANES), 1)
    neg = jnp.float32(-jnp.inf)
    l1 = jnp.where(lane < N_EXPERTS, logits, neg)
    m1 = jnp.max(l1, axis=-1, keepdims=True)
    i1 = jnp.min(jnp.where(l1 == m1, lane, LANES), axis=-1, keepdims=True)
    l2 = jnp.where(lane == i1, neg, l1)
    m2 = jnp.max(l2, axis=-1, keepdims=True)
    i2 = jnp.min(jnp.where(l2 == m2, lane, LANES), axis=-1, keepdims=True)
    e21 = jnp.exp(m2 - m1)
    g1 = 1.0 / (1.0 + e21)
    g2 = e21 / (1.0 + e21)

    oh1 = lane == i1
    oh2 = lane == i2
    both = jnp.where(oh1 | oh2, 1.0, 0.0)
    row = lax.broadcasted_iota(jnp.int32, (TR, TR), 0)
    col = lax.broadcasted_iota(jnp.int32, (TR, TR), 1)
    lower = jnp.where(row > col, 1.0, 0.0).astype(BF16)
    prefix = _dot(lower, both.astype(BF16)) + cnt_ref[...]
    r1 = jnp.sum(jnp.where(oh1, prefix, 0.0), axis=-1, keepdims=True)
    r2 = jnp.sum(jnp.where(oh2, prefix, 0.0), axis=-1, keepdims=True)
    cnt_ref[...] = cnt_ref[...] + jnp.sum(both, axis=0, keepdims=True)

    r1i = r1.astype(jnp.int32)
    r2i = r2.astype(jnp.int32)
    meta = jnp.where(lane == 0, i1, jnp.where(lane == 1, i2, jnp.where(lane == 2, r1i, r2i)))
    meta_ref[...] = meta[:, 0:4]
    gate_ref[...] = jnp.where(lane == 0, g1, g2)[:, 0:2]


def _router(h, gn, rw):
    t, d = h.shape
    return pl.pallas_call(
        _router_kernel,
        grid=(t // TR,),
        in_specs=[
            pl.BlockSpec((TR, d), lambda i: (i, 0)),
            pl.BlockSpec((1, d), lambda i: (0, 0)),
            pl.BlockSpec((d, LANES), lambda i: (0, 0)),
        ],
        out_specs=[
            pl.BlockSpec((TR, d), lambda i: (i, 0)),
            pl.BlockSpec((TR, 4), lambda i: (i, 0)),
            pl.BlockSpec((TR, 2), lambda i: (i, 0)),
            pl.BlockSpec((1, LANES), lambda i: (0, 0)),
        ],
        out_shape=[
            jax.ShapeDtypeStruct((t, d), F32),
            jax.ShapeDtypeStruct((t, 4), jnp.int32),
            jax.ShapeDtypeStruct((t, 2), F32),
            jax.ShapeDtypeStruct((1, LANES), F32),
        ],
        compiler_params=pltpu.CompilerParams(
            dimension_semantics=("arbitrary",), vmem_limit_bytes=VMEM_LIMIT),
        name="moe_router",
    )(h, gn, rw)


def _dispatch_kernel(zblk_ref, zvalid_ref, dest_ref, hn_ref, xg_hbm, zero_ref, sem, zsem):
    @pl.when(pl.program_id(0) == 0)
    def _():
        zero_ref[...] = jnp.zeros_like(zero_ref)
        for j in range(N_ZERO_BLOCKS):
            @pl.when(zvalid_ref[j] != 0)
            def _():
                row0 = pl.multiple_of(zblk_ref[j] * MOE_BLOCK, MOE_BLOCK)
                cp = pltpu.make_async_copy(zero_ref, xg_hbm.at[pl.ds(row0, MOE_BLOCK)], zsem)
                cp.start()
                cp.wait()

    def row_copy(t, k):
        return pltpu.make_async_copy(hn_ref.at[pl.ds(t, 1)],
                                     xg_hbm.at[pl.ds(dest_ref[TOP_K * t + k], 1)], sem)

    def issue(t, carry):
        for k in range(TOP_K):
            row_copy(t, k).start()
        return carry

    lax.fori_loop(0, TD, issue, 0, unroll=DMA_UNROLL)

    def drain(t, carry):
        for k in range(TOP_K):
            row_copy(t, k).wait()
        return carry

    lax.fori_loop(0, TD, drain, 0, unroll=DMA_UNROLL)


def _dispatch(zblk, zvalid, dest_flat, hn, rows):
    t, d = hn.shape
    grid_spec = pltpu.PrefetchScalarGridSpec(
        num_scalar_prefetch=2,
        grid=(t // TD,),
        in_specs=[
            pl.BlockSpec((TOP_K * TD,), lambda i, zb, zv: (i,), memory_space=pltpu.SMEM),
            pl.BlockSpec((TD, d), lambda i, zb, zv: (i, 0)),
        ],
        out_specs=pl.BlockSpec(memory_space=pl.ANY),
        scratch_shapes=[pltpu.VMEM((MOE_BLOCK, d), F32), pltpu.SemaphoreType.DMA,
                        pltpu.SemaphoreType.DMA],
    )
    return pl.pallas_call(
        _dispatch_kernel,
        grid_spec=grid_spec,
        out_shape=jax.ShapeDtypeStruct((rows, d), F32),
        compiler_params=pltpu.CompilerParams(
            dimension_semantics=("arbitrary",), has_side_effects=True,
            vmem_limit_bytes=VMEM_LIMIT),
        name="moe_dispatch",
    )(zblk, zvalid, dest_flat, hn)


def _expert_kernel(blk_e_ref, nact_ref, x_ref, wg_ref, wu_ref, wd_ref, o_ref):
    del blk_e_ref

    @pl.when(pl.program_id(0) < nact_ref[0])
    def _():
        xb = x_ref[...].astype(BF16)
        acc = jnp.zeros((MOE_BLOCK, D_MODEL), F32)
        for c in range(D_EXPERT // EXP_CHUNK):
            cols = slice(c * EXP_CHUNK, (c + 1) * EXP_CHUNK)
            g = _dot(xb, wg_ref[:, cols])
            u = _dot(xb, wu_ref[:, cols])
            acc = acc + _dot((_silu(g) * u).astype(BF16), wd_ref[cols, :])
        o_ref[...] = acc

    @pl.when(pl.program_id(0) >= nact_ref[0])
    def _():
        o_ref[...] = jnp.zeros_like(o_ref)


def _experts(blk_e, nact, xg, wg, wu, wd):
    rows, d = xg.shape
    nblk = rows // MOE_BLOCK

    def row_map(i, be, na):
        return (jnp.minimum(i, na[0] - 1), 0)

    def w_map(i, be, na):
        return (be[i], 0, 0)

    grid_spec = pltpu.PrefetchScalarGridSpec(
        num_scalar_prefetch=2,
        grid=(nblk,),
        in_specs=[
            pl.BlockSpec((MOE_BLOCK, d), row_map),
            pl.BlockSpec((None, d, D_EXPERT), w_map),
            pl.BlockSpec((None, d, D_EXPERT), w_map),
            pl.BlockSpec((None, D_EXPERT, d), w_map),
        ],
        out_specs=pl.BlockSpec((MOE_BLOCK, d), lambda i, be, na: (i, 0)),
    )
    return pl.pallas_call(
        _expert_kernel,
        grid_spec=grid_spec,
        out_shape=jax.ShapeDtypeStruct((rows, d), F32),
        compiler_params=pltpu.CompilerParams(
            dimension_semantics=("arbitrary",), vmem_limit_bytes=VMEM_LIMIT),
        name="moe_experts",
    )(blk_e, nact, xg, wg, wu, wd)


def _combine_kernel(dest_ref, dnext_ref, h_ref, gate_ref, fn_ref, yb_hbm, o_ref, buf_ref, sem):
    i = pl.program_id(0)
    slot = i % 2

    def row_copy(idx_ref, s, t, k):
        return pltpu.make_async_copy(yb_hbm.at[pl.ds(idx_ref[TOP_K * t + k], 1)],
                                     buf_ref.at[s, k, pl.ds(t, 1)], sem.at[s])

    def issue_all(idx_ref, s):
        def issue(t, carry):
            for k in range(TOP_K):
                row_copy(idx_ref, s, t, k).start()
            return carry

        lax.fori_loop(0, TC, issue, 0, unroll=DMA_UNROLL)

    @pl.when(i == 0)
    def _():
        issue_all(dest_ref, 0)

    @pl.when(i + 1 < pl.num_programs(0))
    def _():
        issue_all(dnext_ref, 1 - slot)

    def drain(t, carry):
        for k in range(TOP_K):
            row_copy(dest_ref, slot, t, k).wait()
        return carry

    lax.fori_loop(0, TC, drain, 0, unroll=DMA_UNROLL)

    gates = gate_ref[...]
    h = h_ref[...] + (gates[:, 0:1] * buf_ref[slot, 0] + gates[:, 1:2] * buf_ref[slot, 1])
    o_ref[...] = _rms(h, fn_ref[...])


def _combine(dest_flat, h, gates, fn, yb):
    t, d = h.shape
    last = t // TC - 1
    return pl.pallas_call(
        _combine_kernel,
        grid=(t // TC,),
        in_specs=[
            pl.BlockSpec((TOP_K * TC,), lambda i: (i,), memory_space=pltpu.SMEM),
            pl.BlockSpec((TOP_K * TC,), lambda i: (jnp.minimum(i + 1, last),), memory_space=pltpu.SMEM),
            pl.BlockSpec((TC, d), lambda i: (i, 0)),
            pl.BlockSpec((TC, TOP_K), lambda i: (i, 0)),
            pl.BlockSpec((1, d), lambda i: (0, 0)),
            pl.BlockSpec(memory_space=pl.ANY),
        ],
        out_specs=pl.BlockSpec((TC, d), lambda i: (i, 0)),
        out_shape=jax.ShapeDtypeStruct((t, d), F32),
        scratch_shapes=[pltpu.VMEM((2, TOP_K, TC, d), F32), pltpu.SemaphoreType.DMA((2,))],
        compiler_params=pltpu.CompilerParams(
            dimension_semantics=("arbitrary",), vmem_limit_bytes=VMEM_LIMIT),
        name="moe_combine",
    )(dest_flat, dest_flat, h, gates, fn, yb)


def _retention_tables(s):
    half = RET_DK // 2
    pos = jnp.arange(s, dtype=F32)
    freqs = ROPE_BASE ** (-jnp.arange(half, dtype=F32) / half)
    ang = pos[:, None] * freqs[None, :]
    cos, sin = jnp.cos(ang), jnp.sin(ang)
    cosf = jnp.concatenate([cos, cos], axis=-1)
    sins = jnp.concatenate([-sin, sin], axis=-1)
    log_g = jnp.log1p(-jnp.power(2.0, -5.0 - jnp.arange(RET_HEADS, dtype=F32)))
    idx = jnp.arange(RET_L, dtype=F32)
    ck = jnp.arange(RET_L) // CHUNK
    mask = (ck[:, None] >= ck[None, :]).astype(F32)
    dmat = jnp.exp(log_g[:, None, None] * jnp.abs(idx[:, None] - idx[None, :])) * mask[None]
    qdec = jnp.exp(log_g[:, None] * (idx + 1.0))
    kdec = jnp.exp(log_g[:, None] * (RET_L - 1.0 - idx))
    cdec = jnp.exp(log_g * RET_L)
    qdec = jnp.broadcast_to(qdec[:, :, None], (RET_HEADS, RET_L, RET_DK))
    kdec = jnp.broadcast_to(kdec[:, :, None], (RET_HEADS, RET_L, RET_DK))
    cdec = jnp.broadcast_to(cdec[:, None, None], (RET_HEADS, 1, RET_DV))
    return cosf, sins, dmat, qdec, kdec, cdec


def _moe_plan(meta, counts):
    t = meta.shape[0]
    counts = counts.astype(jnp.int32)
    pcounts = (counts + MOE_BLOCK - 1) // MOE_BLOCK * MOE_BLOCK
    pend = jnp.cumsum(pcounts)
    pstart = pend - pcounts
    dest = pstart[meta[:, 0:TOP_K]] + meta[:, TOP_K:2 * TOP_K]
    rows = (t * TOP_K + MOE_BLOCK - 1) // MOE_BLOCK * MOE_BLOCK + N_EXPERTS * MOE_BLOCK
    nblk = rows // MOE_BLOCK
    nact = pend[-1] // MOE_BLOCK
    blk = jnp.minimum(jnp.arange(nblk, dtype=jnp.int32), nact - 1)
    blk_e = jnp.minimum(jnp.searchsorted(pend, blk * MOE_BLOCK, side='right'), N_EXPERTS - 1)
    tail = nact + jnp.arange(N_ZERO_BLOCKS - N_EXPERTS, dtype=jnp.int32)
    zblk = jnp.concatenate([jnp.maximum(pend // MOE_BLOCK - 1, 0), jnp.minimum(tail, nblk - 1)])
    zvalid = jnp.concatenate([pcounts > counts, tail < nblk])
    return (dest.reshape(-1).astype(jnp.int32), blk_e.astype(jnp.int32),
            nact.reshape(1).astype(jnp.int32), zblk.astype(jnp.int32), zvalid.astype(jnp.int32), rows)


def kernel(x, e_norm_mix, e_w_in, e_ret_gn, e_dw_w, e_dw_b, e_cv_ln_g, e_cv_ln_b, e_w_out, e_norm_ffn, e_ffn_gate, e_ffn_up, e_ffn_down, o_norm_mix, o_w_in, o_sg_ln_g, o_sg_ln_b, o_sg_w, o_sg_b, o_sc_w, o_w_out, o_norm_ffn, o_router, o_exp_gate, o_exp_up, o_exp_down, final_norm):
    bn, s, d = x.shape
    t = bn * s
    row = lambda v: v.reshape(1, -1)
    cosf, sins, dmat, qdec, kdec, cdec = _retention_tables(s)

    h = _l0_mixer(x, row(e_norm_mix[0]), e_w_in[0].astype(BF16), cosf, sins, dmat, qdec, kdec, cdec,
                  row(e_ret_gn[0]), e_dw_w[0], row(e_dw_b[0]), row(e_cv_ln_g[0]), row(e_cv_ln_b[0]),
                  e_w_out[0].astype(BF16))
    h = _ffn(h.reshape(t, d), row(e_norm_ffn[0]), e_ffn_gate[0].astype(BF16),
             e_ffn_up[0].astype(BF16), e_ffn_down[0].astype(BF16))

    sgb = jnp.broadcast_to(o_sg_b[0][:, :, None], (SG_GROUPS, SG_BLOCK, SG_CH // SG_GROUPS))
    h = _l1_mixer(h.reshape(bn, s, d), row(o_norm_mix[0]), o_w_in[0].astype(BF16),
                  row(o_sg_ln_g[0]), row(o_sg_ln_b[0]), o_sg_w[0], sgb, o_sc_w[0],
                  o_w_out[0].astype(BF16)).reshape(t, d)

    rw = jnp.pad(o_router[0], ((0, 0), (0, LANES - N_EXPERTS)))
    hn, meta, gates, counts = _router(h, row(o_norm_ffn[0]), rw)
    dest_flat, blk_e, nact, zblk, zvalid, rows = _moe_plan(meta, counts[0, :N_EXPERTS])
    xg = _dispatch(zblk, zvalid, dest_flat, hn, rows)
    yb = _experts(blk_e, nact, xg, o_exp_gate[0].astype(BF16), o_exp_up[0].astype(BF16),
                  o_exp_down[0].astype(BF16))
    out = _combine(dest_flat, h, gates, row(final_norm), yb)
    return out.reshape(bn, s, d)
```

```python
import functools

import jax
import jax.numpy as jnp
from jax import lax
from jax.experimental import pallas as pl
from jax.experimental.pallas import tpu as pltpu

D_MODEL = 1024
CHUNK = 64
EPS = 1e-6
ROPE_BASE = 10000.0
RET_HEADS = 4
RET_DK = 128
RET_DV = 128
RET_WIDTH = RET_HEADS * RET_DV
CONV_CH = 512
CONV_K = 31
SG_CH = 512
SG_GROUPS = 4
SG_BLOCK = 128
SC_CH = 512
SC_K = 3
D_FF = 2816
N_EXPERTS = 8
TOP_K = 2
D_EXPERT = 3584
EVEN_IN = 2 * RET_HEADS * RET_DK + 2 * RET_WIDTH + 2 * CONV_CH
ODD_IN = 2 * SG_CH + 3 * SC_CH

LANES = 128
SUBLANES = 8
TM = 512
RET_L = 256
CONV_HALO = 32
CONV_ROWS = 64
SC_HALO = 8
FF_CHUNK = 2816
TR = 512
TD = 1024
DMA_UNROLL = 8
TC = 512
MOE_BLOCK = 256
N_ZERO_BLOCKS = 2 * N_EXPERTS
EXP_CHUNK = 1792
VMEM_LIMIT = 60 * 1024 * 1024

BF16 = jnp.bfloat16
F32 = jnp.float32


def _dot(a, b):
    return jnp.dot(a, b, preferred_element_type=F32)


def _sigmoid(x):
    return 1.0 / (1.0 + jnp.exp(-x))


def _silu(x):
    return x * _sigmoid(x)


def _rms(x, g):
    return x * lax.rsqrt(jnp.mean(x * x, axis=-1, keepdims=True) + EPS) * g


def _ln(x, g, b):
    mu = jnp.mean(x, axis=-1, keepdims=True)
    xc = x - mu
    var = jnp.mean(xc * xc, axis=-1, keepdims=True)
    return xc * lax.rsqrt(var + EPS) * g + b


def _l0_mixer_kernel(x_ref, gn_ref, win_ref, cos_ref, sin_ref, dmat_ref, qdec_ref, kdec_ref,
                     cdec_ref, retgn_ref, dww_ref, dwb_ref, lng_ref, lnb_ref, wout_ref,
                     o_ref, state_ref, cbuf_ref, z_ref, zc_ref, ycat_ref):
    @pl.when(pl.program_id(1) == 0)
    def _():
        state_ref[...] = jnp.zeros_like(state_ref)
        cbuf_ref[0:CONV_HALO, :] = jnp.zeros((CONV_HALO, CONV_CH), F32)

    x = x_ref[0]
    hn = _rms(x, gn_ref[...]).astype(BF16)
    nq = RET_HEADS * RET_DK
    n_ret = 2 * nq + 2 * RET_WIDTH
    zc_ref[...] = _dot(hn, win_ref[:, n_ret:])
    z_ref[...] = _dot(hn, win_ref[:, :n_ret])

    cbuf_ref[CONV_HALO:CONV_HALO + TM, :] = zc_ref[:, :CONV_CH] * _sigmoid(zc_ref[:, CONV_CH:])
    first = CONV_HALO - (CONV_K - 1)
    win = CONV_ROWS + CONV_HALO
    for r0 in range(0, TM, CONV_ROWS):
        accs = []
        for l in range(CONV_CH // LANES):
            lanes = slice(l * LANES, (l + 1) * LANES)
            big = cbuf_ref[r0:r0 + win, lanes]
            acc = jnp.zeros((CONV_ROWS, LANES), F32)
            for res in range(SUBLANES):
                rolled = big if res == 0 else pltpu.roll(big, win - res, axis=0)
                for a in range(win // SUBLANES):
                    t = SUBLANES * a + res - first
                    if 0 <= t < CONV_K:
                        acc = acc + rolled[SUBLANES * a:SUBLANES * a + CONV_ROWS, :] * dww_ref[t:t + 1, lanes]
            accs.append(acc)
        cv = _ln(jnp.concatenate(accs, axis=-1) + dwb_ref[...], lng_ref[...], lnb_ref[...])
        ycat_ref[r0:r0 + CONV_ROWS, RET_WIDTH:] = _silu(cv).astype(BF16)
    cbuf_ref[0:CONV_HALO, :] = cbuf_ref[TM:TM + CONV_HALO, :]

    scale = RET_DK ** -0.5
    for j in range(TM // RET_L):
        rows = slice(j * RET_L, (j + 1) * RET_L)
        cosf = cos_ref[rows, :]
        sins = sin_ref[rows, :]
        for h in range(RET_HEADS):
            cols = slice(h * RET_DK, (h + 1) * RET_DK)
            q = z_ref[rows, h * RET_DK:(h + 1) * RET_DK]
            k = z_ref[rows, nq + h * RET_DK:nq + (h + 1) * RET_DK]
            v = z_ref[rows, 2 * nq + h * RET_DV:2 * nq + (h + 1) * RET_DV].astype(BF16)
            g = z_ref[rows, 2 * nq + RET_WIDTH + h * RET_DV:2 * nq + RET_WIDTH + (h + 1) * RET_DV]
            q = (q * cosf + pltpu.roll(q, RET_DK // 2, axis=1) * sins) * scale
            k = k * cosf + pltpu.roll(k, RET_DK // 2, axis=1) * sins
            qb = q.astype(BF16)
            kb = k.astype(BF16)
            sc = lax.dot_general(qb, kb, (((1,), (1,)), ((), ())),
                                 preferred_element_type=F32) * dmat_ref[h]
            st = state_ref[h]
            o = _dot(sc.astype(BF16), v) + _dot((q * qdec_ref[h]).astype(BF16), st.astype(BF16))
            kd_t = (k * kdec_ref[h]).T.astype(BF16)
            state_ref[h] = cdec_ref[h] * st + _dot(kd_t, v)
            mu = jnp.mean(o, axis=-1, keepdims=True)
            oc = o - mu
            var = jnp.mean(oc * oc, axis=-1, keepdims=True)
            r = oc * lax.rsqrt(var + EPS) * retgn_ref[:, cols]
            ycat_ref[rows, cols] = (_silu(g) * r).astype(BF16)

    o_ref[0] = x + _dot(ycat_ref[...], wout_ref[...])


def _l0_mixer(x, gn, win, cosf, sins, dmat, qdec, kdec, cdec, retgn, dww, dwb, lng, lnb, wout):
    bn, s, d = x.shape
    full = lambda shape: pl.BlockSpec(shape, lambda b, i: (0,) * len(shape))
    return pl.pallas_call(
        _l0_mixer_kernel,
        grid=(bn, s // TM),
        in_specs=[
            pl.BlockSpec((1, TM, d), lambda b, i: (b, i, 0)),
            full((1, d)),
            full((d, EVEN_IN)),
            pl.BlockSpec((TM, RET_DK), lambda b, i: (i, 0)),
            pl.BlockSpec((TM, RET_DK), lambda b, i: (i, 0)),
            full((RET_HEADS, RET_L, RET_L)),
            full((RET_HEADS, RET_L, RET_DK)),
            full((RET_HEADS, RET_L, RET_DK)),
            full((RET_HEADS, 1, RET_DV)),
            full((1, RET_WIDTH)),
            full((CONV_K, CONV_CH)),
            full((1, CONV_CH)),
            full((1, CONV_CH)),
            full((1, CONV_CH)),
            full((RET_WIDTH + CONV_CH, d)),
        ],
        out_specs=pl.BlockSpec((1, TM, d), lambda b, i: (b, i, 0)),
        out_shape=jax.ShapeDtypeStruct((bn, s, d), F32),
        scratch_shapes=[
            pltpu.VMEM((RET_HEADS, RET_DK, RET_DV), F32),
            pltpu.VMEM((TM + CONV_HALO, CONV_CH), F32),
            pltpu.VMEM((TM, EVEN_IN - 2 * CONV_CH), F32),
            pltpu.VMEM((TM, 2 * CONV_CH), F32),
            pltpu.VMEM((TM, RET_WIDTH + CONV_CH), BF16),
        ],
        compiler_params=pltpu.CompilerParams(
            dimension_semantics=("arbitrary", "arbitrary"), vmem_limit_bytes=VMEM_LIMIT),
        name="l0_mixer",
    )(x, gn, win, cosf, sins, dmat, qdec, kdec, cdec, retgn, dww, dwb, lng, lnb, wout)


def _ffn_kernel(h_ref, gn_ref, wg_ref, wu_ref, wd_ref, o_ref):
    h = h_ref[...]
    hn = _rms(h, gn_ref[...]).astype(BF16)
    acc = h
    for c in range(D_FF // FF_CHUNK):
        cols = slice(c * FF_CHUNK, (c + 1) * FF_CHUNK)
        g = _dot(hn, wg_ref[:, cols])
        u = _dot(hn, wu_ref[:, cols])
        acc = acc + _dot((_silu(g) * u).astype(BF16), wd_ref[cols, :])
    o_ref[...] = acc


def _ffn(h, gn, wg, wu, wd):
    t, d = h.shape
    full = lambda shape: pl.BlockSpec(shape, lambda i: (0,) * len(shape))
    return pl.pallas_call(
        _ffn_kernel,
        grid=(t // TM,),
        in_specs=[
            pl.BlockSpec((TM, d), lambda i: (i, 0)),
            full((1, d)),
            full((d, D_FF)),
            full((d, D_FF)),
            full((D_FF, d)),
        ],
        out_specs=pl.BlockSpec((TM, d), lambda i: (i, 0)),
        out_shape=jax.ShapeDtypeStruct((t, d), F32),
        compiler_params=pltpu.CompilerParams(
            dimension_semantics=("arbitrary",), vmem_limit_bytes=VMEM_LIMIT),
        name="ffn_swiglu",
    )(h, gn, wg, wu, wd)


def _l1_mixer_kernel(x_ref, gn_ref, win_ref, lng_ref, lnb_ref, sgw_ref, sgb_ref, scw_ref, wout_ref,
                     o_ref, pbuf_ref, z_ref, ycat_ref):
    @pl.when(pl.program_id(1) == 0)
    def _():
        pbuf_ref[0:SC_HALO, :] = jnp.zeros((SC_HALO, SC_CH), F32)

    x = x_ref[0]
    hn = _rms(x, gn_ref[...]).astype(BF16)
    z_ref[...] = _dot(hn, win_ref[...])

    vn = _ln(jax.nn.gelu(z_ref[:, SG_CH:2 * SG_CH]), lng_ref[...], lnb_ref[...]).astype(BF16)
    gw = SG_CH // SG_GROUPS
    pc = lax.broadcasted_iota(jnp.int32, (SG_BLOCK, SG_BLOCK), 0) // CHUNK
    qc = lax.broadcasted_iota(jnp.int32, (SG_BLOCK, SG_BLOCK), 1) // CHUNK
    for g in range(SG_GROUPS):
        cols = slice(g * gw, (g + 1) * gw)
        ws = jnp.where(pc >= qc, sgw_ref[g], 0.0).astype(BF16)
        for m in range(TM // SG_BLOCK):
            rows = slice(m * SG_BLOCK, (m + 1) * SG_BLOCK)
            sp = _dot(ws, vn[rows, cols]) + sgb_ref[g]
            u = jax.nn.gelu(z_ref[rows, g * gw:(g + 1) * gw])
            ycat_ref[rows, cols] = (u * sp).astype(BF16)

    base = 2 * SG_CH
    bg = z_ref[:, base:base + SC_CH]
    cg = z_ref[:, base + SC_CH:base + 2 * SC_CH]
    hv = z_ref[:, base + 2 * SC_CH:]
    pbuf_ref[SC_HALO:SC_HALO + TM, :] = cg * hv
    first = SC_HALO - (SC_K - 1)
    conv = jnp.zeros((TM, SC_CH), F32)
    for t in range(SC_K):
        conv = conv + pbuf_ref[first + t:first + t + TM, :] * scw_ref[t:t + 1, :]
    ycat_ref[:, SG_CH:] = (bg * conv).astype(BF16)
    pbuf_ref[0:SC_HALO, :] = pbuf_ref[TM:TM + SC_HALO, :]

    o_ref[0] = x + _dot(ycat_ref[...], wout_ref[...])


def _l1_mixer(x, gn, win, lng, lnb, sgw, sgb, scw, wout):
    bn, s, d = x.shape
    full = lambda shape: pl.BlockSpec(shape, lambda b, i: (0,) * len(shape))
    return pl.pallas_call(
        _l1_mixer_kernel,
        grid=(bn, s // TM),
        in_specs=[
            pl.BlockSpec((1, TM, d), lambda b, i: (b, i, 0)),
            full((1, d)),
            full((d, ODD_IN)),
            full((1, SG_CH)),
            full((1, SG_CH)),
            full((SG_GROUPS, SG_BLOCK, SG_BLOCK)),
            full((SG_GROUPS, SG_BLOCK, SG_CH // SG_GROUPS)),
            full((SC_K, SC_CH)),
            full((SG_CH + SC_CH, d)),
        ],
        out_specs=pl.BlockSpec((1, TM, d), lambda b, i: (b, i, 0)),
        out_shape=jax.ShapeDtypeStruct((bn, s, d), F32),
        scratch_shapes=[
            pltpu.VMEM((TM + SC_HALO, SC_CH), F32),
            pltpu.VMEM((TM, ODD_IN), F32),
            pltpu.VMEM((TM, SG_CH + SC_CH), BF16),
        ],
        compiler_params=pltpu.CompilerParams(
            dimension_semantics=("arbitrary", "arbitrary"), vmem_limit_bytes=VMEM_LIMIT),
        name="l1_mixer",
    )(x, gn, win, lng, lnb, sgw, sgb, scw, wout)


def _router_kernel(h_ref, gn_ref, rw_ref, hn_ref, meta_ref, gate_ref, cnt_ref):
    @pl.when(pl.program_id(0) == 0)
    def _():
        cnt_ref[...] = jnp.zeros_like(cnt_ref)

    hn = _rms(h_ref[...], gn_ref[...])
    hn_ref[...] = hn
    hn_hi = hn.astype(BF16)
    hn_lo = (hn - hn_hi.astype(F32)).astype(BF16)
    rw = rw_ref[...]
    rw_hi = rw.astype(BF16)
    rw_lo = (rw - rw_hi.astype(F32)).astype(BF16)
    logits = _dot(hn_hi, rw_hi) + (_dot(hn_hi, rw_lo) + _dot(hn_lo, rw_hi))

    lane = lax.broadcasted_iota(jnp.int32, (TR, LANES), 1)
    neg = jnp.float32(-jnp.inf)
    l1 = jnp.where(lane < N_EXPERTS, logits, neg)
    m1 = jnp.max(l1, axis=-1, keepdims=True)
    i1 = jnp.min(jnp.where(l1 == m1, lane, LANES), axis=-1, keepdims=True)
    l2 = jnp.where(lane == i1, neg, l1)
    m2 = jnp.max(l2, axis=-1, keepdims=True)
    i2 = jnp.min(jnp.where(l2 == m2, lane, LANES), axis=-1, keepdims=True)
    e21 = jnp.exp(m2 - m1)
    g1 = 1.0 / (1.0 + e21)
    g2 = e21 / (1.0 + e21)

    oh1 = lane == i1
    oh2 = lane == i2
    both = jnp.where(oh1 | oh2, 1.0, 0.0)
    row = lax.broadcasted_iota(jnp.int32, (TR, TR), 0)
    col = lax.broadcasted_iota(jnp.int32, (TR, TR), 1)
    lower = jnp.where(row > col, 1.0, 0.0).astype(BF16)
    prefix = _dot(lower, both.astype(BF16)) + cnt_ref[...]
    r1 = jnp.sum(jnp.where(oh1, prefix, 0.0), axis=-1, keepdims=True)
    r2 = jnp.sum(jnp.where(oh2, prefix, 0.0), axis=-1, keepdims=True)
    cnt_ref[...] = cnt_ref[...] + jnp.sum(both, axis=0, keepdims=True)

    r1i = r1.astype(jnp.int32)
    r2i = r2.astype(jnp.int32)
    meta = jnp.where(lane == 0, i1, jnp.where(lane == 1, i2, jnp.where(lane == 2, r1i, r2i)))
    meta_ref[...] = meta[:, 0:4]
    gate_ref[...] = jnp.where(lane == 0, g1, g2)[:, 0:2]


def _router(h, gn, rw):
    t, d = h.shape
    return pl.pallas_call(
        _router_kernel,
        grid=(t // TR,),
        in_specs=[
            pl.BlockSpec((TR, d), lambda i: (i, 0)),
            pl.BlockSpec((1, d), lambda i: (0, 0)),
            pl.BlockSpec((d, LANES), lambda i: (0, 0)),
        ],
        out_specs=[
            pl.BlockSpec((TR, d), lambda i: (i, 0)),
            pl.BlockSpec((TR, 4), lambda i: (i, 0)),
            pl.BlockSpec((TR, 2), lambda i: (i, 0)),
            pl.BlockSpec((1, LANES), lambda i: (0, 0)),
        ],
        out_shape=[
            jax.ShapeDtypeStruct((t, d), F32),
            jax.ShapeDtypeStruct((t, 4), jnp.int32),
            jax.ShapeDtypeStruct((t, 2), F32),
            jax.ShapeDtypeStruct((1, LANES), F32),
        ],
        compiler_params=pltpu.CompilerParams(
            dimension_semantics=("arbitrary",), vmem_limit_bytes=VMEM_LIMIT),
        name="moe_router",
    )(h, gn, rw)


def _dispatch_kernel(zblk_ref, zvalid_ref, dest_ref, hn_ref, xg_hbm, zero_ref, sem, zsem):
    @pl.when(pl.program_id(0) == 0)
    def _():
        zero_ref[...] = jnp.zeros_like(zero_ref)
        for j in range(N_ZERO_BLOCKS):
            @pl.when(zvalid_ref[j] != 0)
            def _():
                row0 = pl.multiple_of(zblk_ref[j] * MOE_BLOCK, MOE_BLOCK)
                cp = pltpu.make_async_copy(zero_ref, xg_hbm.at[pl.ds(row0, MOE_BLOCK)], zsem)
                cp.start()
                cp.wait()

    def row_copy(t, k):
        return pltpu.make_async_copy(hn_ref.at[pl.ds(t, 1)],
                                     xg_hbm.at[pl.ds(dest_ref[TOP_K * t + k], 1)], sem)

    def issue(t, carry):
        for k in range(TOP_K):
            row_copy(t, k).start()
        return carry

    lax.fori_loop(0, TD, issue, 0, unroll=DMA_UNROLL)

    def drain(t, carry):
        for k in range(TOP_K):
            row_copy(t, k).wait()
        return carry

    lax.fori_loop(0, TD, drain, 0, unroll=DMA_UNROLL)


def _dispatch(zblk, zvalid, dest_flat, hn, rows):
    t, d = hn.shape
    grid_spec = pltpu.PrefetchScalarGridSpec(
        num_scalar_prefetch=2,
        grid=(t // TD,),
        in_specs=[
            pl.BlockSpec((TOP_K * TD,), lambda i, zb, zv: (i,), memory_space=pltpu.SMEM),
            pl.BlockSpec((TD, d), lambda i, zb, zv: (i, 0)),
        ],
        out_specs=pl.BlockSpec(memory_space=pl.ANY),
        scratch_shapes=[pltpu.VMEM((MOE_BLOCK, d), F32), pltpu.SemaphoreType.DMA,
                        pltpu.SemaphoreType.DMA],
    )
    return pl.pallas_call(
        _dispatch_kernel,
        grid_spec=grid_spec,
        out_shape=jax.ShapeDtypeStruct((rows, d), F32),
        compiler_params=pltpu.CompilerParams(
            dimension_semantics=("arbitrary",), has_side_effects=True,
            vmem_limit_bytes=VMEM_LIMIT),
        name="moe_dispatch",
    )(zblk, zvalid, dest_flat, hn)


def _expert_kernel(blk_e_ref, nact_ref, x_ref, wg_ref, wu_ref, wd_ref, o_ref):
    del blk_e_ref

    @pl.when(pl.program_id(0) < nact_ref[0])
    def _():
        xb = x_ref[...].astype(BF16)
        acc = jnp.zeros((MOE_BLOCK, D_MODEL), F32)
        for c in range(D_EXPERT // EXP_CHUNK):
            cols = slice(c * EXP_CHUNK, (c + 1) * EXP_CHUNK)
            g = _dot(xb, wg_ref[:, cols])
            u = _dot(xb, wu_ref[:, cols])
            acc = acc + _dot((_silu(g) * u).astype(BF16), wd_ref[cols, :])
        o_ref[...] = acc

    @pl.when(pl.program_id(0) >= nact_ref[0])
    def _():
        o_ref[...] = jnp.zeros_like(o_ref)


def _experts(blk_e, nact, xg, wg, wu, wd):
    rows, d = xg.shape
    nblk = rows // MOE_BLOCK

    def row_map(i, be, na):
        return (jnp.minimum(i, na[0] - 1), 0)

    def w_map(i, be, na):
        return (be[i], 0, 0)

    grid_spec = pltpu.PrefetchScalarGridSpec(
        num_scalar_prefetch=2,
        grid=(nblk,),
        in_specs=[
            pl.BlockSpec((MOE_BLOCK, d), row_map),
            pl.BlockSpec((None, d, D_EXPERT), w_map),
            pl.BlockSpec((None, d, D_EXPERT), w_map),
            pl.BlockSpec((None, D_EXPERT, d), w_map),
        ],
        out_specs=pl.BlockSpec((MOE_BLOCK, d), lambda i, be, na: (i, 0)),
    )
    return pl.pallas_call(
        _expert_kernel,
        grid_spec=grid_spec,
        out_shape=jax.ShapeDtypeStruct((rows, d), F32),
        compiler_params=pltpu.CompilerParams(
            dimension_semantics=("arbitrary",), vmem_limit_bytes=VMEM_LIMIT),
        name="moe_experts",
    )(blk_e, nact, xg, wg, wu, wd)


def _combine_kernel(dest_ref, dnext_ref, h_ref, gate_ref, fn_ref, yb_hbm, o_ref, buf_ref, sem):
    i = pl.program_id(0)
    slot = i % 2

    def row_copy(idx_ref, s, t, k):
        return pltpu.make_async_copy(yb_hbm.at[pl.ds(idx_ref[TOP_K * t + k], 1)],
                                     buf_ref.at[s, k, pl.ds(t, 1)], sem.at[s])

    def issue_all(idx_ref, s):
        def issue(t, carry):
            for k in range(TOP_K):
                row_copy(idx_ref, s, t, k).start()
            return carry

        lax.fori_loop(0, TC, issue, 0, unroll=DMA_UNROLL)

    @pl.when(i == 0)
    def _():
        issue_all(dest_ref, 0)

    @pl.when(i + 1 < pl.num_programs(0))
    def _():
        issue_all(dnext_ref, 1 - slot)

    def drain(t, carry):
        for k in range(TOP_K):
            row_copy(dest_ref, slot, t, k).wait()
        return carry

    lax.fori_loop(0, TC, drain, 0, unroll=DMA_UNROLL)

    gates = gate_ref[...]
    h = h_ref[...] + (gates[:, 0:1] * buf_ref[slot, 0] + gates[:, 1:2] * buf_ref[slot, 1])
    o_ref[...] = _rms(h, fn_ref[...])


def _combine(dest_flat, h, gates, fn, yb):
    t, d = h.shape
    last = t // TC - 1
    return pl.pallas_call(
        _combine_kernel,
        grid=(t // TC,),
        in_specs=[
            pl.BlockSpec((TOP_K * TC,), lambda i: (i,), memory_space=pltpu.SMEM),
            pl.BlockSpec((TOP_K * TC,), lambda i: (jnp.minimum(i + 1, last),), memory_space=pltpu.SMEM),
            pl.BlockSpec((TC, d), lambda i: (i, 0)),
            pl.BlockSpec((TC, TOP_K), lambda i: (i, 0)),
            pl.BlockSpec((1, d), lambda i: (0, 0)),
            pl.BlockSpec(memory_space=pl.ANY),
        ],
        out_specs=pl.BlockSpec((TC, d), lambda i: (i, 0)),
        out_shape=jax.ShapeDtypeStruct((t, d), F32),
        scratch_shapes=[pltpu.VMEM((2, TOP_K, TC, d), F32), pltpu.SemaphoreType.DMA((2,))],
        compiler_params=pltpu.CompilerParams(
            dimension_semantics=("arbitrary",), vmem_limit_bytes=VMEM_LIMIT),
        name="moe_combine",
    )(dest_flat, dest_flat, h, gates, fn, yb)


def _retention_tables(s):
    half = RET_DK // 2
    pos = jnp.arange(s, dtype=F32)
    freqs = ROPE_BASE ** (-jnp.arange(half, dtype=F32) / half)
    ang = pos[:, None] * freqs[None, :]
    cos, sin = jnp.cos(ang), jnp.sin(ang)
    cosf = jnp.concatenate([cos, cos], axis=-1)
    sins = jnp.concatenate([-sin, sin], axis=-1)
    log_g = jnp.log1p(-jnp.power(2.0, -5.0 - jnp.arange(RET_HEADS, dtype=F32)))
    idx = jnp.arange(RET_L, dtype=F32)
    ck = jnp.arange(RET_L) // CHUNK
    mask = (ck[:, None] >= ck[None, :]).astype(F32)
    dmat = jnp.exp(log_g[:, None, None] * jnp.abs(idx[:, None] - idx[None, :])) * mask[None]
    qdec = jnp.exp(log_g[:, None] * (idx + 1.0))
    kdec = jnp.exp(log_g[:, None] * (RET_L - 1.0 - idx))
    cdec = jnp.exp(log_g * RET_L)
    qdec = jnp.broadcast_to(qdec[:, :, None], (RET_HEADS, RET_L, RET_DK))
    kdec = jnp.broadcast_to(kdec[:, :, None], (RET_HEADS, RET_L, RET_DK))
    cdec = jnp.broadcast_to(cdec[:, None, None], (RET_HEADS, 1, RET_DV))
    return cosf, sins, dmat, qdec, kdec, cdec


def _moe_plan(meta, counts):
    t = meta.shape[0]
    counts = counts.astype(jnp.int32)
    pcounts = (counts + MOE_BLOCK - 1) // MOE_BLOCK * MOE_BLOCK
    pend = jnp.cumsum(pcounts)
    pstart = pend - pcounts
    dest = pstart[meta[:, 0:TOP_K]] + meta[:, TOP_K:2 * TOP_K]
    rows = (t * TOP_K + MOE_BLOCK - 1) // MOE_BLOCK * MOE_BLOCK + N_EXPERTS * MOE_BLOCK
    nblk = rows // MOE_BLOCK
    nact = pend[-1] // MOE_BLOCK
    blk = jnp.minimum(jnp.arange(nblk, dtype=jnp.int32), nact - 1)
    blk_e = jnp.minimum(jnp.searchsorted(pend, blk * MOE_BLOCK, side='right'), N_EXPERTS - 1)
    tail = nact + jnp.arange(N_ZERO_BLOCKS - N_EXPERTS, dtype=jnp.int32)
    zblk = jnp.concatenate([jnp.maximum(pend // MOE_BLOCK - 1, 0), jnp.minimum(tail, nblk - 1)])
    zvalid = jnp.concatenate([pcounts > counts, tail < nblk])
    return (dest.reshape(-1).astype(jnp.int32), blk_e.astype(jnp.int32),
            nact.reshape(1).astype(jnp.int32), zblk.astype(jnp.int32), zvalid.astype(jnp.int32), rows)


def kernel(x, e_norm_mix, e_w_in, e_ret_gn, e_dw_w, e_dw_b, e_cv_ln_g, e_cv_ln_b, e_w_out, e_norm_ffn, e_ffn_gate, e_ffn_up, e_ffn_down, o_norm_mix, o_w_in, o_sg_ln_g, o_sg_ln_b, o_sg_w, o_sg_b, o_sc_w, o_w_out, o_norm_ffn, o_router, o_exp_gate, o_exp_up, o_exp_down, final_norm):
    bn, s, d = x.shape
    t = bn * s
    row = lambda v: v.reshape(1, -1)
    cosf, sins, dmat, qdec, kdec, cdec = _retention_tables(s)

    h = _l0_mixer(x, row(e_norm_mix[0]), e_w_in[0].astype(BF16), cosf, sins, dmat, qdec, kdec, cdec,
                  row(e_ret_gn[0]), e_dw_w[0], row(e_dw_b[0]), row(e_cv_ln_g[0]), row(e_cv_ln_b[0]),
                  e_w_out[0].astype(BF16))
    h = _ffn(h.reshape(t, d), row(e_norm_ffn[0]), e_ffn_gate[0].astype(BF16),
             e_ffn_up[0].astype(BF16), e_ffn_down[0].astype(BF16))

    sgb = jnp.broadcast_to(o_sg_b[0][:, :, None], (SG_GROUPS, SG_BLOCK, SG_CH // SG_GROUPS))
    h = _l1_mixer(h.reshape(bn, s, d), row(o_norm_mix[0]), o_w_in[0].astype(BF16),
                  row(o_sg_ln_g[0]), row(o_sg_ln_b[0]), o_sg_w[0], sgb, o_sc_w[0],
                  o_w_out[0].astype(BF16)).reshape(t, d)

    rw = jnp.pad(o_router[0], ((0, 0), (0, LANES - N_EXPERTS)))
    hn, meta, gates, counts = _router(h, row(o_norm_ffn[0]), rw)
    dest_flat, blk_e, nact, zblk, zvalid, rows = _moe_plan(meta, counts[0, :N_EXPERTS])
    xg = _dispatch(zblk, zvalid, dest_flat, hn, rows)
    yb = _experts(blk_e, nact, xg, o_exp_gate[0].astype(BF16), o_exp_up[0].astype(BF16),
                  o_exp_down[0].astype(BF16))
    out = _combine(dest_flat, h, gates, row(final_norm), yb)
    return out.reshape(bn, s, d)
```

```python
import functools

import jax
import jax.numpy as jnp
from jax import lax
from jax.experimental import pallas as pl
from jax.experimental.pallas import tpu as pltpu

D_MODEL = 1024
CHUNK = 64
EPS = 1e-6
ROPE_BASE = 10000.0
RET_HEADS = 4
RET_DK = 128
RET_DV = 128
RET_WIDTH = RET_HEADS * RET_DV
CONV_CH = 512
CONV_K = 31
SG_CH = 512
SG_GROUPS = 4
SG_BLOCK = 128
SC_CH = 512
SC_K = 3
D_FF = 2816
N_EXPERTS = 8
TOP_K = 2
D_EXPERT = 3584
EVEN_IN = 2 * RET_HEADS * RET_DK + 2 * RET_WIDTH + 2 * CONV_CH
ODD_IN = 2 * SG_CH + 3 * SC_CH

LANES = 128
SUBLANES = 8
TM = 512
RET_L = 256
CONV_HALO = 32
CONV_ROWS = 64
SC_HALO = 8
FF_CHUNK = 2816
TR = 512
TD = 1024
DMA_UNROLL = 8
TC = 512
MOE_BLOCK = 256
N_ZERO_BLOCKS = 2 * N_EXPERTS
EXP_CHUNK = 1792
VMEM_LIMIT = 60 * 1024 * 1024

BF16 = jnp.bfloat16
F32 = jnp.float32


def _dot(a, b):
    return jnp.dot(a, b, preferred_element_type=F32)


def _sigmoid(x):
    return 1.0 / (1.0 + jnp.exp(-x))


def _silu(x):
    return x * _sigmoid(x)


def _rms(x, g):
    return x * lax.rsqrt(jnp.mean(x * x, axis=-1, keepdims=True) + EPS) * g


def _ln(x, g, b):
    mu = jnp.mean(x, axis=-1, keepdims=True)
    xc = x - mu
    var = jnp.mean(xc * xc, axis=-1, keepdims=True)
    return xc * lax.rsqrt(var + EPS) * g + b


def _l0_mixer_kernel(x_ref, gn_ref, win_ref, cos_ref, sin_ref, dmat_ref, qdec_ref, kdec_ref,
                     cdec_ref, retgn_ref, dww_ref, dwb_ref, lng_ref, lnb_ref, wout_ref,
                     o_ref, state_ref, cbuf_ref, z_ref, zc_ref, ycat_ref):
    @pl.when(pl.program_id(1) == 0)
    def _():
        state_ref[...] = jnp.zeros_like(state_ref)
        cbuf_ref[0:CONV_HALO, :] = jnp.zeros((CONV_HALO, CONV_CH), F32)

    x = x_ref[0]
    hn = _rms(x, gn_ref[...]).astype(BF16)
    nq = RET_HEADS * RET_DK
    n_ret = 2 * nq + 2 * RET_WIDTH
    zc_ref[...] = _dot(hn, win_ref[:, n_ret:])
    z_ref[...] = _dot(hn, win_ref[:, :n_ret])

    cbuf_ref[CONV_HALO:CONV_HALO + TM, :] = zc_ref[:, :CONV_CH] * _sigmoid(zc_ref[:, CONV_CH:])
    first = CONV_HALO - (CONV_K - 1)
    win = CONV_ROWS + CONV_HALO
    for r0 in range(0, TM, CONV_ROWS):
        accs = []
        for l in range(CONV_CH // LANES):
            lanes = slice(l * LANES, (l + 1) * LANES)
            big = cbuf_ref[r0:r0 + win, lanes]
            acc = jnp.zeros((CONV_ROWS, LANES), F32)
            for res in range(SUBLANES):
                rolled = big if res == 0 else pltpu.roll(big, win - res, axis=0)
                for a in range(win // SUBLANES):
                    t = SUBLANES * a + res - first
                    if 0 <= t < CONV_K:
                        acc = acc + rolled[SUBLANES * a:SUBLANES * a + CONV_ROWS, :] * dww_ref[t:t + 1, lanes]
            accs.append(acc)
        cv = _ln(jnp.concatenate(accs, axis=-1) + dwb_ref[...], lng_ref[...], lnb_ref[...])
        ycat_ref[r0:r0 + CONV_ROWS, RET_WIDTH:] = _silu(cv).astype(BF16)
    cbuf_ref[0:CONV_HALO, :] = cbuf_ref[TM:TM + CONV_HALO, :]

    scale = RET_DK ** -0.5
    for j in range(TM // RET_L):
        rows = slice(j * RET_L, (j + 1) * RET_L)
        cosf = cos_ref[rows, :]
        sins = sin_ref[rows, :]
        for h in range(RET_HEADS):
            cols = slice(h * RET_DK, (h + 1) * RET_DK)
            q = z_ref[rows, h * RET_DK:(h + 1) * RET_DK]
            k = z_ref[rows, nq + h * RET_DK:nq + (h + 1) * RET_DK]
            v = z_ref[rows, 2 * nq + h * RET_DV:2 * nq + (h + 1) * RET_DV].astype(BF16)
            g = z_ref[rows, 2 * nq + RET_WIDTH + h * RET_DV:2 * nq + RET_WIDTH + (h + 1) * RET_DV]
            q = (q * cosf + pltpu.roll(q, RET_DK // 2, axis=1) * sins) * scale
            k = k * cosf + pltpu.roll(k, RET_DK // 2, axis=1) * sins
            qb = q.astype(BF16)
            kb = k.astype(BF16)
            sc = lax.dot_general(qb, kb, (((1,), (1,)), ((), ())),
                                 preferred_element_type=F32) * dmat_ref[h]
            st = state_ref[h]
            o = _dot(sc.astype(BF16), v) + _dot((q * qdec_ref[h]).astype(BF16), st.astype(BF16))
            kd_t = (k * kdec_ref[h]).T.astype(BF16)
            state_ref[h] = cdec_ref[h] * st + _dot(kd_t, v)
            mu = jnp.mean(o, axis=-1, keepdims=True)
            oc = o - mu
            var = jnp.mean(oc * oc, axis=-1, keepdims=True)
            r = oc * lax.rsqrt(var + EPS) * retgn_ref[:, cols]
            ycat_ref[rows, cols] = (_silu(g) * r).astype(BF16)

    o_ref[0] = x + _dot(ycat_ref[...], wout_ref[...])


def _l0_mixer(x, gn, win, cosf, sins, dmat, qdec, kdec, cdec, retgn, dww, dwb, lng, lnb, wout):
    bn, s, d = x.shape
    full = lambda shape: pl.BlockSpec(shape, lambda b, i: (0,) * len(shape))
    return pl.pallas_call(
        _l0_mixer_kernel,
        grid=(bn, s // TM),
        in_specs=[
            pl.BlockSpec((1, TM, d), lambda b, i: (b, i, 0)),
            full((1, d)),
            full((d, EVEN_IN)),
            pl.BlockSpec((TM, RET_DK), lambda b, i: (i, 0)),
            pl.BlockSpec((TM, RET_DK), lambda b, i: (i, 0)),
            full((RET_HEADS, RET_L, RET_L)),
            full((RET_HEADS, RET_L, RET_DK)),
            full((RET_HEADS, RET_L, RET_DK)),
            full((RET_HEADS, 1, RET_DV)),
            full((1, RET_WIDTH)),
            full((CONV_K, CONV_CH)),
            full((1, CONV_CH)),
            full((1, CONV_CH)),
            full((1, CONV_CH)),
            full((RET_WIDTH + CONV_CH, d)),
        ],
        out_specs=pl.BlockSpec((1, TM, d), lambda b, i: (b, i, 0)),
        out_shape=jax.ShapeDtypeStruct((bn, s, d), F32),
        scratch_shapes=[
            pltpu.VMEM((RET_HEADS, RET_DK, RET_DV), F32),
            pltpu.VMEM((TM + CONV_HALO, CONV_CH), F32),
            pltpu.VMEM((TM, EVEN_IN - 2 * CONV_CH), F32),
            pltpu.VMEM((TM, 2 * CONV_CH), F32),
            pltpu.VMEM((TM, RET_WIDTH + CONV_CH), BF16),
        ],
        compiler_params=pltpu.CompilerParams(
            dimension_semantics=("arbitrary", "arbitrary"), vmem_limit_bytes=VMEM_LIMIT),
        name="l0_mixer",
    )(x, gn, win, cosf, sins, dmat, qdec, kdec, cdec, retgn, dww, dwb, lng, lnb, wout)


def _ffn_kernel(h_ref, gn_ref, wg_ref, wu_ref, wd_ref, o_ref):
    h = h_ref[...]
    hn = _rms(h, gn_ref[...]).astype(BF16)
    acc = h
    for c in range(D_FF // FF_CHUNK):
        cols = slice(c * FF_CHUNK, (c + 1) * FF_CHUNK)
        g = _dot(hn, wg_ref[:, cols])
        u = _dot(hn, wu_ref[:, cols])
        acc = acc + _dot((_silu(g) * u).astype(BF16), wd_ref[cols, :])
    o_ref[...] = acc


def _ffn(h, gn, wg, wu, wd):
    t, d = h.shape
    full = lambda shape: pl.BlockSpec(shape, lambda i: (0,) * len(shape))
    return pl.pallas_call(
        _ffn_kernel,
        grid=(t // TM,),
        in_specs=[
            pl.BlockSpec((TM, d), lambda i: (i, 0)),
            full((1, d)),
            full((d, D_FF)),
            full((d, D_FF)),
            full((D_FF, d)),
        ],
        out_specs=pl.BlockSpec((TM, d), lambda i: (i, 0)),
        out_shape=jax.ShapeDtypeStruct((t, d), F32),
        compiler_params=pltpu.CompilerParams(
            dimension_semantics=("arbitrary",), vmem_limit_bytes=VMEM_LIMIT),
        name="ffn_swiglu",
    )(h, gn, wg, wu, wd)


def _l1_mixer_kernel(x_ref, gn_ref, win_ref, lng_ref, lnb_ref, sgw_ref, sgb_ref, scw_ref, wout_ref,
                     o_ref, pbuf_ref, z_ref, ycat_ref):
    @pl.when(pl.program_id(1) == 0)
    def _():
        pbuf_ref[0:SC_HALO, :] = jnp.zeros((SC_HALO, SC_CH), F32)

    x = x_ref[0]
    hn = _rms(x, gn_ref[...]).astype(BF16)
    z_ref[...] = _dot(hn, win_ref[...])

    vn = _ln(jax.nn.gelu(z_ref[:, SG_CH:2 * SG_CH]), lng_ref[...], lnb_ref[...]).astype(BF16)
    gw = SG_CH // SG_GROUPS
    pc = lax.broadcasted_iota(jnp.int32, (SG_BLOCK, SG_BLOCK), 0) // CHUNK
    qc = lax.broadcasted_iota(jnp.int32, (SG_BLOCK, SG_BLOCK), 1) // CHUNK
    for g in range(SG_GROUPS):
        cols = slice(g * gw, (g + 1) * gw)
        ws = jnp.where(pc >= qc, sgw_ref[g], 0.0).astype(BF16)
        for m in range(TM // SG_BLOCK):
            rows = slice(m * SG_BLOCK, (m + 1) * SG_BLOCK)
            sp = _dot(ws, vn[rows, cols]) + sgb_ref[g]
            u = jax.nn.gelu(z_ref[rows, g * gw:(g + 1) * gw])
            ycat_ref[rows, cols] = (u * sp).astype(BF16)

    base = 2 * SG_CH
    bg = z_ref[:, base:base + SC_CH]
    cg = z_ref[:, base + SC_CH:base + 2 * SC_CH]
    hv = z_ref[:, base + 2 * SC_CH:]
    pbuf_ref[SC_HALO:SC_HALO + TM, :] = cg * hv
    first = SC_HALO - (SC_K - 1)
    conv = jnp.zeros((TM, SC_CH), F32)
    for t in range(SC_K):
        conv = conv + pbuf_ref[first + t:first + t + TM, :] * scw_ref[t:t + 1, :]
    ycat_ref[:, SG_CH:] = (bg * conv).astype(BF16)
    pbuf_ref[0:SC_HALO, :] = pbuf_ref[TM:TM + SC_HALO, :]

    o_ref[0] = x + _dot(ycat_ref[...], wout_ref[...])


def _l1_mixer(x, gn, win, lng, lnb, sgw, sgb, scw, wout):
    bn, s, d = x.shape
    full = lambda shape: pl.BlockSpec(shape, lambda b, i: (0,) * len(shape))
    return pl.pallas_call(
        _l1_mixer_kernel,
        grid=(bn, s // TM),
        in_specs=[
            pl.BlockSpec((1, TM, d), lambda b, i: (b, i, 0)),
            full((1, d)),
            full((d, ODD_IN)),
            full((1, SG_CH)),
            full((1, SG_CH)),
            full((SG_GROUPS, SG_BLOCK, SG_BLOCK)),
            full((SG_GROUPS, SG_BLOCK, SG_CH // SG_GROUPS)),
            full((SC_K, SC_CH)),
            full((SG_CH + SC_CH, d)),
        ],
        out_specs=pl.BlockSpec((1, TM, d), lambda b, i: (b, i, 0)),
        out_shape=jax.ShapeDtypeStruct((bn, s, d), F32),
        scratch_shapes=[
            pltpu.VMEM((TM + SC_HALO, SC_CH), F32),
            pltpu.VMEM((TM, ODD_IN), F32),
            pltpu.VMEM((TM, SG_CH + SC_CH), BF16),
        ],
        compiler_params=pltpu.CompilerParams(
            dimension_semantics=("arbitrary", "arbitrary"), vmem_limit_bytes=VMEM_LIMIT),
        name="l1_mixer",
    )(x, gn, win, lng, lnb, sgw, sgb, scw, wout)


def _router_kernel(h_ref, gn_ref, rw_ref, hn_ref, meta_ref, gate_ref, cnt_ref):
    @pl.when(pl.program_id(0) == 0)
    def _():
        cnt_ref[...] = jnp.zeros_like(cnt_ref)

    hn = _rms(h_ref[...], gn_ref[...])
    hn_ref[...] = hn
    hn_hi = hn.astype(BF16)
    hn_lo = (hn - hn_hi.astype(F32)).astype(BF16)
    rw = rw_ref[...]
    rw_hi = rw.astype(BF16)
    rw_lo = (rw - rw_hi.astype(F32)).astype(BF16)
    logits = _dot(hn_hi, rw_hi) + (_dot(hn_hi, rw_lo) + _dot(hn_lo, rw_hi))

    lane = lax.broadcasted_iota(jnp.int32, (TR, LANES), 1)
    neg = jnp.float32(-jnp.inf)
    l1 = jnp.where(lane < N_EXPERTS, logits, neg)
    m1 = jnp.max(l1, axis=-1, keepdims=True)
    i1 = jnp.min(jnp.where(l1 == m1, lane, LANES), axis=-1, keepdims=True)
    l2 = jnp.where(lane == i1, neg, l1)
    m2 = jnp.max(l2, axis=-1, keepdims=True)
    i2 = jnp.min(jnp.where(l2 == m2, lane, LANES), axis=-1, keepdims=True)
    e21 = jnp.exp(m2 - m1)
    g1 = 1.0 / (1.0 + e21)
    g2 = e21 / (1.0 + e21)

    oh1 = lane == i1
    oh2 = lane == i2
    both = jnp.where(oh1 | oh2, 1.0, 0.0)
    row = lax.broadcasted_iota(jnp.int32, (TR, TR), 0)
    col = lax.broadcasted_iota(jnp.int32, (TR, TR), 1)
    lower = jnp.where(row > col, 1.0, 0.0).astype(BF16)
    prefix = _dot(lower, both.astype(BF16)) + cnt_ref[...]
    r1 = jnp.sum(jnp.where(oh1, prefix, 0.0), axis=-1, keepdims=True)
    r2 = jnp.sum(jnp.where(oh2, prefix, 0.0), axis=-1, keepdims=True)
    cnt_ref[...] = cnt_ref[...] + jnp.sum(both, axis=0, keepdims=True)

    r1i = r1.astype(jnp.int32)
    r2i = r2.astype(jnp.int32)
    meta = jnp.where(lane == 0, i1, jnp.where(lane == 1, i2, jnp.where(lane == 2, r1i, r2i)))
    meta_ref[...] = meta.T[0:SUBLANES, :]
    gate_ref[...] = jnp.where(lane == 0, g1, g2)[:, 0:2]


def _router(h, gn, rw):
    t, d = h.shape
    return pl.pallas_call(
        _router_kernel,
        grid=(t // TR,),
        in_specs=[
            pl.BlockSpec((TR, d), lambda i: (i, 0)),
            pl.BlockSpec((1, d), lambda i: (0, 0)),
            pl.BlockSpec((d, LANES), lambda i: (0, 0)),
        ],
        out_specs=[
            pl.BlockSpec((TR, d), lambda i: (i, 0)),
            pl.BlockSpec((SUBLANES, TR), lambda i: (0, i)),
            pl.BlockSpec((TR, 2), lambda i: (i, 0)),
            pl.BlockSpec((1, LANES), lambda i: (0, 0)),
        ],
        out_shape=[
            jax.ShapeDtypeStruct((t, d), F32),
            jax.ShapeDtypeStruct((SUBLANES, t), jnp.int32),
            jax.ShapeDtypeStruct((t, 2), F32),
            jax.ShapeDtypeStruct((1, LANES), F32),
        ],
        compiler_params=pltpu.CompilerParams(
            dimension_semantics=("arbitrary",), vmem_limit_bytes=VMEM_LIMIT),
        name="moe_router",
    )(h, gn, rw)


def _dispatch_kernel(zblk_ref, zvalid_ref, d0_ref, d1_ref, hn_ref, xg_hbm, zero_ref, sem, zsem):
    dest_refs = (d0_ref, d1_ref)
    @pl.when(pl.program_id(0) == 0)
    def _():
        zero_ref[...] = jnp.zeros_like(zero_ref)
        for j in range(N_ZERO_BLOCKS):
            @pl.when(zvalid_ref[j] != 0)
            def _():
                row0 = pl.multiple_of(zblk_ref[j] * MOE_BLOCK, MOE_BLOCK)
                cp = pltpu.make_async_copy(zero_ref, xg_hbm.at[pl.ds(row0, MOE_BLOCK)], zsem)
                cp.start()
                cp.wait()

    def row_copy(t, k):
        return pltpu.make_async_copy(hn_ref.at[pl.ds(t, 1)],
                                     xg_hbm.at[pl.ds(dest_refs[k][t], 1)], sem)

    def issue(t, carry):
        for k in range(TOP_K):
            row_copy(t, k).start()
        return carry

    lax.fori_loop(0, TD, issue, 0, unroll=DMA_UNROLL)

    def drain(t, carry):
        for k in range(TOP_K):
            row_copy(t, k).wait()
        return carry

    lax.fori_loop(0, TD, drain, 0, unroll=DMA_UNROLL)


def _dispatch(zblk, zvalid, dest, hn, rows):
    t, d = hn.shape
    grid_spec = pltpu.PrefetchScalarGridSpec(
        num_scalar_prefetch=2,
        grid=(t // TD,),
        in_specs=[
            pl.BlockSpec((TD,), lambda i, zb, zv: (i,), memory_space=pltpu.SMEM),
            pl.BlockSpec((TD,), lambda i, zb, zv: (i,), memory_space=pltpu.SMEM),
            pl.BlockSpec((TD, d), lambda i, zb, zv: (i, 0)),
        ],
        out_specs=pl.BlockSpec(memory_space=pl.ANY),
        scratch_shapes=[pltpu.VMEM((MOE_BLOCK, d), F32), pltpu.SemaphoreType.DMA,
                        pltpu.SemaphoreType.DMA],
    )
    return pl.pallas_call(
        _dispatch_kernel,
        grid_spec=grid_spec,
        out_shape=jax.ShapeDtypeStruct((rows, d), F32),
        compiler_params=pltpu.CompilerParams(
            dimension_semantics=("arbitrary",), has_side_effects=True,
            vmem_limit_bytes=VMEM_LIMIT),
        name="moe_dispatch",
    )(zblk, zvalid, dest[0], dest[1], hn)


def _expert_kernel(blk_e_ref, nact_ref, x_ref, wg_ref, wu_ref, wd_ref, o_ref):
    del blk_e_ref

    @pl.when(pl.program_id(0) < nact_ref[0])
    def _():
        xb = x_ref[...].astype(BF16)
        acc = jnp.zeros((MOE_BLOCK, D_MODEL), F32)
        for c in range(D_EXPERT // EXP_CHUNK):
            cols = slice(c * EXP_CHUNK, (c + 1) * EXP_CHUNK)
            g = _dot(xb, wg_ref[:, cols])
            u = _dot(xb, wu_ref[:, cols])
            acc = acc + _dot((_silu(g) * u).astype(BF16), wd_ref[cols, :])
        o_ref[...] = acc

    @pl.when(pl.program_id(0) >= nact_ref[0])
    def _():
        o_ref[...] = jnp.zeros_like(o_ref)


def _experts(blk_e, nact, xg, wg, wu, wd):
    rows, d = xg.shape
    nblk = rows // MOE_BLOCK

    def row_map(i, be, na):
        return (jnp.minimum(i, na[0] - 1), 0)

    def w_map(i, be, na):
        return (be[i], 0, 0)

    grid_spec = pltpu.PrefetchScalarGridSpec(
        num_scalar_prefetch=2,
        grid=(nblk,),
        in_specs=[
            pl.BlockSpec((MOE_BLOCK, d), row_map),
            pl.BlockSpec((None, d, D_EXPERT), w_map),
            pl.BlockSpec((None, d, D_EXPERT), w_map),
            pl.BlockSpec((None, D_EXPERT, d), w_map),
        ],
        out_specs=pl.BlockSpec((MOE_BLOCK, d), lambda i, be, na: (i, 0)),
    )
    return pl.pallas_call(
        _expert_kernel,
        grid_spec=grid_spec,
        out_shape=jax.ShapeDtypeStruct((rows, d), F32),
        compiler_params=pltpu.CompilerParams(
            dimension_semantics=("arbitrary",), vmem_limit_bytes=VMEM_LIMIT),
        name="moe_experts",
    )(blk_e, nact, xg, wg, wu, wd)


def _combine_kernel(d0_ref, d1_ref, n0_ref, n1_ref, h_ref, gate_ref, fn_ref, yb_hbm, o_ref, buf_ref,
                    sem):
    dest_ref = (d0_ref, d1_ref)
    dnext_ref = (n0_ref, n1_ref)
    i = pl.program_id(0)
    slot = i % 2

    def row_copy(idx_ref, s, t, k):
        return pltpu.make_async_copy(yb_hbm.at[pl.ds(idx_ref[k][t], 1)],
                                     buf_ref.at[s, k, pl.ds(t, 1)], sem.at[s])

    def issue_all(idx_ref, s):
        def issue(t, carry):
            for k in range(TOP_K):
                row_copy(idx_ref, s, t, k).start()
            return carry

        lax.fori_loop(0, TC, issue, 0, unroll=DMA_UNROLL)

    @pl.when(i == 0)
    def _():
        issue_all(dest_ref, 0)

    @pl.when(i + 1 < pl.num_programs(0))
    def _():
        issue_all(dnext_ref, 1 - slot)

    def drain(t, carry):
        for k in range(TOP_K):
            row_copy(dest_ref, slot, t, k).wait()
        return carry

    lax.fori_loop(0, TC, drain, 0, unroll=DMA_UNROLL)

    gates = gate_ref[...]
    h = h_ref[...] + (gates[:, 0:1] * buf_ref[slot, 0] + gates[:, 1:2] * buf_ref[slot, 1])
    o_ref[...] = _rms(h, fn_ref[...])


def _combine(dest, h, gates, fn, yb):
    t, d = h.shape
    last = t // TC - 1
    return pl.pallas_call(
        _combine_kernel,
        grid=(t // TC,),
        in_specs=[
            pl.BlockSpec((TC,), lambda i: (i,), memory_space=pltpu.SMEM),
            pl.BlockSpec((TC,), lambda i: (i,), memory_space=pltpu.SMEM),
            pl.BlockSpec((TC,), lambda i: (jnp.minimum(i + 1, last),), memory_space=pltpu.SMEM),
            pl.BlockSpec((TC,), lambda i: (jnp.minimum(i + 1, last),), memory_space=pltpu.SMEM),
            pl.BlockSpec((TC, d), lambda i: (i, 0)),
            pl.BlockSpec((TC, TOP_K), lambda i: (i, 0)),
            pl.BlockSpec((1, d), lambda i: (0, 0)),
            pl.BlockSpec(memory_space=pl.ANY),
        ],
        out_specs=pl.BlockSpec((TC, d), lambda i: (i, 0)),
        out_shape=jax.ShapeDtypeStruct((t, d), F32),
        scratch_shapes=[pltpu.VMEM((2, TOP_K, TC, d), F32), pltpu.SemaphoreType.DMA((2,))],
        compiler_params=pltpu.CompilerParams(
            dimension_semantics=("arbitrary",), vmem_limit_bytes=VMEM_LIMIT),
        name="moe_combine",
    )(dest[0], dest[1], dest[0], dest[1], h, gates, fn, yb)


def _retention_tables(s):
    half = RET_DK // 2
    pos = jnp.arange(s, dtype=F32)
    freqs = ROPE_BASE ** (-jnp.arange(half, dtype=F32) / half)
    ang = pos[:, None] * freqs[None, :]
    cos, sin = jnp.cos(ang), jnp.sin(ang)
    cosf = jnp.concatenate([cos, cos], axis=-1)
    sins = jnp.concatenate([-sin, sin], axis=-1)
    log_g = jnp.log1p(-jnp.power(2.0, -5.0 - jnp.arange(RET_HEADS, dtype=F32)))
    idx = jnp.arange(RET_L, dtype=F32)
    ck = jnp.arange(RET_L) // CHUNK
    mask = (ck[:, None] >= ck[None, :]).astype(F32)
    dmat = jnp.exp(log_g[:, None, None] * jnp.abs(idx[:, None] - idx[None, :])) * mask[None]
    qdec = jnp.exp(log_g[:, None] * (idx + 1.0))
    kdec = jnp.exp(log_g[:, None] * (RET_L - 1.0 - idx))
    cdec = jnp.exp(log_g * RET_L)
    qdec = jnp.broadcast_to(qdec[:, :, None], (RET_HEADS, RET_L, RET_DK))
    kdec = jnp.broadcast_to(kdec[:, :, None], (RET_HEADS, RET_L, RET_DK))
    cdec = jnp.broadcast_to(cdec[:, None, None], (RET_HEADS, 1, RET_DV))
    return cosf, sins, dmat, qdec, kdec, cdec


def _moe_plan(meta_t, counts):
    t = meta_t.shape[1]
    counts = counts.astype(jnp.int32)
    pcounts = (counts + MOE_BLOCK - 1) // MOE_BLOCK * MOE_BLOCK
    pend = jnp.cumsum(pcounts)
    pstart = pend - pcounts
    experts = meta_t[0:TOP_K]
    base = jnp.zeros_like(experts)
    for e in range(N_EXPERTS):
        base = jnp.where(experts == e, pstart[e], base)
    dest = base + meta_t[TOP_K:2 * TOP_K]
    rows = (t * TOP_K + MOE_BLOCK - 1) // MOE_BLOCK * MOE_BLOCK + N_EXPERTS * MOE_BLOCK
    nblk = rows // MOE_BLOCK
    nact = pend[-1] // MOE_BLOCK
    blk = jnp.minimum(jnp.arange(nblk, dtype=jnp.int32), nact - 1)
    blk_e = jnp.sum((pend[None, :] <= (blk * MOE_BLOCK)[:, None]).astype(jnp.int32), axis=1)
    blk_e = jnp.minimum(blk_e, N_EXPERTS - 1)
    tail = nact + jnp.arange(N_ZERO_BLOCKS - N_EXPERTS, dtype=jnp.int32)
    zblk = jnp.concatenate([jnp.maximum(pend // MOE_BLOCK - 1, 0), jnp.minimum(tail, nblk - 1)])
    zvalid = jnp.concatenate([pcounts > counts, tail < nblk])
    return (dest.astype(jnp.int32), blk_e.astype(jnp.int32), nact.reshape(1).astype(jnp.int32),
            zblk.astype(jnp.int32), zvalid.astype(jnp.int32), rows)


def kernel(x, e_norm_mix, e_w_in, e_ret_gn, e_dw_w, e_dw_b, e_cv_ln_g, e_cv_ln_b, e_w_out, e_norm_ffn, e_ffn_gate, e_ffn_up, e_ffn_down, o_norm_mix, o_w_in, o_sg_ln_g, o_sg_ln_b, o_sg_w, o_sg_b, o_sc_w, o_w_out, o_norm_ffn, o_router, o_exp_gate, o_exp_up, o_exp_down, final_norm):
    bn, s, d = x.shape
    t = bn * s
    row = lambda v: v.reshape(1, -1)
    cosf, sins, dmat, qdec, kdec, cdec = _retention_tables(s)

    h = _l0_mixer(x, row(e_norm_mix[0]), e_w_in[0].astype(BF16), cosf, sins, dmat, qdec, kdec, cdec,
                  row(e_ret_gn[0]), e_dw_w[0], row(e_dw_b[0]), row(e_cv_ln_g[0]), row(e_cv_ln_b[0]),
                  e_w_out[0].astype(BF16))
    h = _ffn(h.reshape(t, d), row(e_norm_ffn[0]), e_ffn_gate[0].astype(BF16),
             e_ffn_up[0].astype(BF16), e_ffn_down[0].astype(BF16))

    sgb = jnp.broadcast_to(o_sg_b[0][:, :, None], (SG_GROUPS, SG_BLOCK, SG_CH // SG_GROUPS))
    h = _l1_mixer(h.reshape(bn, s, d), row(o_norm_mix[0]), o_w_in[0].astype(BF16),
                  row(o_sg_ln_g[0]), row(o_sg_ln_b[0]), o_sg_w[0], sgb, o_sc_w[0],
                  o_w_out[0].astype(BF16)).reshape(t, d)

    rw = jnp.pad(o_router[0], ((0, 0), (0, LANES - N_EXPERTS)))
    hn, meta, gates, counts = _router(h, row(o_norm_ffn[0]), rw)
    dest, blk_e, nact, zblk, zvalid, rows = _moe_plan(meta, counts[0, :N_EXPERTS])
    xg = _dispatch(zblk, zvalid, dest, hn, rows)
    yb = _experts(blk_e, nact, xg, o_exp_gate[0].astype(BF16), o_exp_up[0].astype(BF16),
                  o_exp_down[0].astype(BF16))
    out = _combine(dest, h, gates, row(final_norm), yb)
    return out.reshape(bn, s, d)
```

```python
import functools

import jax
import jax.numpy as jnp
from jax import lax
from jax.experimental import pallas as pl
from jax.experimental.pallas import tpu as pltpu

D_MODEL = 1024
CHUNK = 64
EPS = 1e-6
ROPE_BASE = 10000.0
RET_HEADS = 4
RET_DK = 128
RET_DV = 128
RET_WIDTH = RET_HEADS * RET_DV
CONV_CH = 512
CONV_K = 31
SG_CH = 512
SG_GROUPS = 4
SG_BLOCK = 128
SC_CH = 512
SC_K = 3
D_FF = 2816
N_EXPERTS = 8
TOP_K = 2
D_EXPERT = 3584
EVEN_IN = 2 * RET_HEADS * RET_DK + 2 * RET_WIDTH + 2 * CONV_CH
ODD_IN = 2 * SG_CH + 3 * SC_CH

LANES = 128
SUBLANES = 8
TM = 512
RET_L = 256
CONV_HALO = 32
CONV_ROWS = 64
SC_HALO = 8
FF_CHUNK = 2816
TR = 512
TD = 1024
DMA_UNROLL = 8
TC = 512
MOE_BLOCK = 256
EXP_CHUNK = 1792
CAST_UNIT = 256
UNITS_PER_MAT = D_EXPERT // CAST_UNIT
N_UNITS = 3 * UNITS_PER_MAT
UNIT_GROUP = 7
VMEM_LIMIT = 60 * 1024 * 1024

BF16 = jnp.bfloat16
F32 = jnp.float32


def _dot(a, b):
    return jnp.dot(a, b, preferred_element_type=F32)


def _sigmoid(x):
    return 1.0 / (1.0 + jnp.exp(-x))


def _silu(x):
    return x * _sigmoid(x)


def _rms(x, g):
    return x * lax.rsqrt(jnp.mean(x * x, axis=-1, keepdims=True) + EPS) * g


def _ln(x, g, b):
    mu = jnp.mean(x, axis=-1, keepdims=True)
    xc = x - mu
    var = jnp.mean(xc * xc, axis=-1, keepdims=True)
    return xc * lax.rsqrt(var + EPS) * g + b


def _l0_mixer_kernel(x_ref, gn_ref, win_ref, cos_ref, sin_ref, dmat_ref, qdec_ref, kdec_ref,
                     cdec_ref, retgn_ref, dww_ref, dwb_ref, lng_ref, lnb_ref, wout_ref,
                     o_ref, state_ref, cbuf_ref, z_ref, zc_ref, ycat_ref):
    @pl.when(pl.program_id(1) == 0)
    def _():
        state_ref[...] = jnp.zeros_like(state_ref)
        cbuf_ref[0:CONV_HALO, :] = jnp.zeros((CONV_HALO, CONV_CH), F32)

    x = x_ref[0]
    hn = _rms(x, gn_ref[...]).astype(BF16)
    nq = RET_HEADS * RET_DK
    n_ret = 2 * nq + 2 * RET_WIDTH
    zc_ref[...] = _dot(hn, win_ref[:, n_ret:])
    z_ref[...] = _dot(hn, win_ref[:, :n_ret])

    cbuf_ref[CONV_HALO:CONV_HALO + TM, :] = zc_ref[:, :CONV_CH] * _sigmoid(zc_ref[:, CONV_CH:])
    first = CONV_HALO - (CONV_K - 1)
    win = CONV_ROWS + CONV_HALO
    for r0 in range(0, TM, CONV_ROWS):
        accs = []
        for l in range(CONV_CH // LANES):
            lanes = slice(l * LANES, (l + 1) * LANES)
            big = cbuf_ref[r0:r0 + win, lanes]
            acc = jnp.zeros((CONV_ROWS, LANES), F32)
            for res in range(SUBLANES):
                rolled = big if res == 0 else pltpu.roll(big, win - res, axis=0)
                for a in range(win // SUBLANES):
                    t = SUBLANES * a + res - first
                    if 0 <= t < CONV_K:
                        acc = acc + rolled[SUBLANES * a:SUBLANES * a + CONV_ROWS, :] * dww_ref[t:t + 1, lanes]
            accs.append(acc)
        cv = _ln(jnp.concatenate(accs, axis=-1) + dwb_ref[...], lng_ref[...], lnb_ref[...])
        ycat_ref[r0:r0 + CONV_ROWS, RET_WIDTH:] = _silu(cv).astype(BF16)
    cbuf_ref[0:CONV_HALO, :] = cbuf_ref[TM:TM + CONV_HALO, :]

    scale = RET_DK ** -0.5
    for j in range(TM // RET_L):
        rows = slice(j * RET_L, (j + 1) * RET_L)
        cosf = cos_ref[rows, :]
        sins = sin_ref[rows, :]
        for h in range(RET_HEADS):
            cols = slice(h * RET_DK, (h + 1) * RET_DK)
            q = z_ref[rows, h * RET_DK:(h + 1) * RET_DK]
            k = z_ref[rows, nq + h * RET_DK:nq + (h + 1) * RET_DK]
            v = z_ref[rows, 2 * nq + h * RET_DV:2 * nq + (h + 1) * RET_DV].astype(BF16)
            g = z_ref[rows, 2 * nq + RET_WIDTH + h * RET_DV:2 * nq + RET_WIDTH + (h + 1) * RET_DV]
            q = (q * cosf + pltpu.roll(q, RET_DK // 2, axis=1) * sins) * scale
            k = k * cosf + pltpu.roll(k, RET_DK // 2, axis=1) * sins
            qb = q.astype(BF16)
            kb = k.astype(BF16)
            sc = lax.dot_general(qb, kb, (((1,), (1,)), ((), ())),
                                 preferred_element_type=F32) * dmat_ref[h]
            st = state_ref[h]
            o = _dot(sc.astype(BF16), v) + _dot((q * qdec_ref[h]).astype(BF16), st.astype(BF16))
            kd_t = (k * kdec_ref[h]).T.astype(BF16)
            state_ref[h] = cdec_ref[h] * st + _dot(kd_t, v)
            mu = jnp.mean(o, axis=-1, keepdims=True)
            oc = o - mu
            var = jnp.mean(oc * oc, axis=-1, keepdims=True)
            r = oc * lax.rsqrt(var + EPS) * retgn_ref[:, cols]
            ycat_ref[rows, cols] = (_silu(g) * r).astype(BF16)

    o_ref[0] = x + _dot(ycat_ref[...], wout_ref[...])


def _l0_mixer(x, gn, win, cosf, sins, dmat, qdec, kdec, cdec, retgn, dww, dwb, lng, lnb, wout):
    bn, s, d = x.shape
    full = lambda shape: pl.BlockSpec(shape, lambda b, i: (0,) * len(shape))
    return pl.pallas_call(
        _l0_mixer_kernel,
        grid=(bn, s // TM),
        in_specs=[
            pl.BlockSpec((1, TM, d), lambda b, i: (b, i, 0)),
            full((1, d)),
            full((d, EVEN_IN)),
            pl.BlockSpec((TM, RET_DK), lambda b, i: (i, 0)),
            pl.BlockSpec((TM, RET_DK), lambda b, i: (i, 0)),
            full((RET_HEADS, RET_L, RET_L)),
            full((RET_HEADS, RET_L, RET_DK)),
            full((RET_HEADS, RET_L, RET_DK)),
            full((RET_HEADS, 1, RET_DV)),
            full((1, RET_WIDTH)),
            full((CONV_K, CONV_CH)),
            full((1, CONV_CH)),
            full((1, CONV_CH)),
            full((1, CONV_CH)),
            full((RET_WIDTH + CONV_CH, d)),
        ],
        out_specs=pl.BlockSpec((1, TM, d), lambda b, i: (b, i, 0)),
        out_shape=jax.ShapeDtypeStruct((bn, s, d), F32),
        scratch_shapes=[
            pltpu.VMEM((RET_HEADS, RET_DK, RET_DV), F32),
            pltpu.VMEM((TM + CONV_HALO, CONV_CH), F32),
            pltpu.VMEM((TM, EVEN_IN - 2 * CONV_CH), F32),
            pltpu.VMEM((TM, 2 * CONV_CH), F32),
            pltpu.VMEM((TM, RET_WIDTH + CONV_CH), BF16),
        ],
        compiler_params=pltpu.CompilerParams(
            dimension_semantics=("arbitrary", "arbitrary"), vmem_limit_bytes=VMEM_LIMIT),
        name="l0_mixer",
    )(x, gn, win, cosf, sins, dmat, qdec, kdec, cdec, retgn, dww, dwb, lng, lnb, wout)


def _ffn_kernel(h_ref, gn_ref, wg_ref, wu_ref, wd_ref, o_ref):
    h = h_ref[...]
    hn = _rms(h, gn_ref[...]).astype(BF16)
    acc = h
    for c in range(D_FF // FF_CHUNK):
        cols = slice(c * FF_CHUNK, (c + 1) * FF_CHUNK)
        g = _dot(hn, wg_ref[:, cols])
        u = _dot(hn, wu_ref[:, cols])
        acc = acc + _dot((_silu(g) * u).astype(BF16), wd_ref[cols, :])
    o_ref[...] = acc


def _ffn(h, gn, wg, wu, wd):
    t, d = h.shape
    full = lambda shape: pl.BlockSpec(shape, lambda i: (0,) * len(shape))
    return pl.pallas_call(
        _ffn_kernel,
        grid=(t // TM,),
        in_specs=[
            pl.BlockSpec((TM, d), lambda i: (i, 0)),
            full((1, d)),
            full((d, D_FF)),
            full((d, D_FF)),
            full((D_FF, d)),
        ],
        out_specs=pl.BlockSpec((TM, d), lambda i: (i, 0)),
        out_shape=jax.ShapeDtypeStruct((t, d), F32),
        compiler_params=pltpu.CompilerParams(
            dimension_semantics=("arbitrary",), vmem_limit_bytes=VMEM_LIMIT),
        name="ffn_swiglu",
    )(h, gn, wg, wu, wd)


def _l1_mixer_kernel(x_ref, gn_ref, win_ref, lng_ref, lnb_ref, sgw_ref, sgb_ref, scw_ref, wout_ref,
                     o_ref, pbuf_ref, z_ref, ycat_ref):
    @pl.when(pl.program_id(1) == 0)
    def _():
        pbuf_ref[0:SC_HALO, :] = jnp.zeros((SC_HALO, SC_CH), F32)

    x = x_ref[0]
    hn = _rms(x, gn_ref[...]).astype(BF16)
    z_ref[...] = _dot(hn, win_ref[...])

    vn = _ln(jax.nn.gelu(z_ref[:, SG_CH:2 * SG_CH]), lng_ref[...], lnb_ref[...]).astype(BF16)
    gw = SG_CH // SG_GROUPS
    pc = lax.broadcasted_iota(jnp.int32, (SG_BLOCK, SG_BLOCK), 0) // CHUNK
    qc = lax.broadcasted_iota(jnp.int32, (SG_BLOCK, SG_BLOCK), 1) // CHUNK
    for g in range(SG_GROUPS):
        cols = slice(g * gw, (g + 1) * gw)
        ws = jnp.where(pc >= qc, sgw_ref[g], 0.0).astype(BF16)
        for m in range(TM // SG_BLOCK):
            rows = slice(m * SG_BLOCK, (m + 1) * SG_BLOCK)
            sp = _dot(ws, vn[rows, cols]) + sgb_ref[g]
            u = jax.nn.gelu(z_ref[rows, g * gw:(g + 1) * gw])
            ycat_ref[rows, cols] = (u * sp).astype(BF16)

    base = 2 * SG_CH
    bg = z_ref[:, base:base + SC_CH]
    cg = z_ref[:, base + SC_CH:base + 2 * SC_CH]
    hv = z_ref[:, base + 2 * SC_CH:]
    pbuf_ref[SC_HALO:SC_HALO + TM, :] = cg * hv
    first = SC_HALO - (SC_K - 1)
    conv = jnp.zeros((TM, SC_CH), F32)
    for t in range(SC_K):
        conv = conv + pbuf_ref[first + t:first + t + TM, :] * scw_ref[t:t + 1, :]
    ycat_ref[:, SG_CH:] = (bg * conv).astype(BF16)
    pbuf_ref[0:SC_HALO, :] = pbuf_ref[TM:TM + SC_HALO, :]

    o_ref[0] = x + _dot(ycat_ref[...], wout_ref[...])


def _l1_mixer(x, gn, win, lng, lnb, sgw, sgb, scw, wout):
    bn, s, d = x.shape
    full = lambda shape: pl.BlockSpec(shape, lambda b, i: (0,) * len(shape))
    return pl.pallas_call(
        _l1_mixer_kernel,
        grid=(bn, s // TM),
        in_specs=[
            pl.BlockSpec((1, TM, d), lambda b, i: (b, i, 0)),
            full((1, d)),
            full((d, ODD_IN)),
            full((1, SG_CH)),
            full((1, SG_CH)),
            full((SG_GROUPS, SG_BLOCK, SG_BLOCK)),
            full((SG_GROUPS, SG_BLOCK, SG_CH // SG_GROUPS)),
            full((SC_K, SC_CH)),
            full((SG_CH + SC_CH, d)),
        ],
        out_specs=pl.BlockSpec((1, TM, d), lambda b, i: (b, i, 0)),
        out_shape=jax.ShapeDtypeStruct((bn, s, d), F32),
        scratch_shapes=[
            pltpu.VMEM((TM + SC_HALO, SC_CH), F32),
            pltpu.VMEM((TM, ODD_IN), F32),
            pltpu.VMEM((TM, SG_CH + SC_CH), BF16),
        ],
        compiler_params=pltpu.CompilerParams(
            dimension_semantics=("arbitrary", "arbitrary"), vmem_limit_bytes=VMEM_LIMIT),
        name="l1_mixer",
    )(x, gn, win, lng, lnb, sgw, sgb, scw, wout)


def _router_kernel(h_ref, gn_ref, rw_ref, hn_ref, meta_ref, gate_ref, cnt_ref):
    @pl.when(pl.program_id(0) == 0)
    def _():
        cnt_ref[...] = jnp.zeros_like(cnt_ref)

    hn = _rms(h_ref[...], gn_ref[...])
    hn_ref[...] = hn
    hn_hi = hn.astype(BF16)
    hn_lo = (hn - hn_hi.astype(F32)).astype(BF16)
    rw = rw_ref[...]
    rw_hi = rw.astype(BF16)
    rw_lo = (rw - rw_hi.astype(F32)).astype(BF16)
    logits = _dot(hn_hi, rw_hi) + (_dot(hn_hi, rw_lo) + _dot(hn_lo, rw_hi))

    lane = lax.broadcasted_iota(jnp.int32, (TR, LANES), 1)
    neg = jnp.float32(-jnp.inf)
    l1 = jnp.where(lane < N_EXPERTS, logits, neg)
    m1 = jnp.max(l1, axis=-1, keepdims=True)
    i1 = jnp.min(jnp.where(l1 == m1, lane, LANES), axis=-1, keepdims=True)
    l2 = jnp.where(lane == i1, neg, l1)
    m2 = jnp.max(l2, axis=-1, keepdims=True)
    i2 = jnp.min(jnp.where(l2 == m2, lane, LANES), axis=-1, keepdims=True)
    e21 = jnp.exp(m2 - m1)
    g1 = 1.0 / (1.0 + e21)
    g2 = e21 / (1.0 + e21)

    oh1 = lane == i1
    oh2 = lane == i2
    both = jnp.where(oh1 | oh2, 1.0, 0.0)
    row = lax.broadcasted_iota(jnp.int32, (TR, TR), 0)
    col = lax.broadcasted_iota(jnp.int32, (TR, TR), 1)
    lower = jnp.where(row > col, 1.0, 0.0).astype(BF16)
    prefix = _dot(lower, both.astype(BF16)) + cnt_ref[...]
    r1 = jnp.sum(jnp.where(oh1, prefix, 0.0), axis=-1, keepdims=True)
    r2 = jnp.sum(jnp.where(oh2, prefix, 0.0), axis=-1, keepdims=True)
    cnt_ref[...] = cnt_ref[...] + jnp.sum(both, axis=0, keepdims=True)

    r1i = r1.astype(jnp.int32)
    r2i = r2.astype(jnp.int32)
    meta = jnp.where(lane == 0, i1, jnp.where(lane == 1, i2, jnp.where(lane == 2, r1i, r2i)))
    meta_ref[...] = meta.T[0:SUBLANES, :]
    gate_ref[...] = jnp.where(lane == 0, g1, g2)[:, 0:2]


def _router(h, gn, rw):
    t, d = h.shape
    return pl.pallas_call(
        _router_kernel,
        grid=(t // TR,),
        in_specs=[
            pl.BlockSpec((TR, d), lambda i: (i, 0)),
            pl.BlockSpec((1, d), lambda i: (0, 0)),
            pl.BlockSpec((d, LANES), lambda i: (0, 0)),
        ],
        out_specs=[
            pl.BlockSpec((TR, d), lambda i: (i, 0)),
            pl.BlockSpec((SUBLANES, TR), lambda i: (0, i)),
            pl.BlockSpec((TR, 2), lambda i: (i, 0)),
            pl.BlockSpec((1, LANES), lambda i: (0, 0)),
        ],
        out_shape=[
            jax.ShapeDtypeStruct((t, d), F32),
            jax.ShapeDtypeStruct((SUBLANES, t), jnp.int32),
            jax.ShapeDtypeStruct((t, 2), F32),
            jax.ShapeDtypeStruct((1, LANES), F32),
        ],
        compiler_params=pltpu.CompilerParams(
            dimension_semantics=("arbitrary",), vmem_limit_bytes=VMEM_LIMIT),
        name="moe_router",
    )(h, gn, rw)


def _dispatch_kernel(zb_e_ref, zv_e_ref, zb_t_ref, zv_t_ref, d0_ref, d1_ref, hn_ref, xg_hbm,
                     zero_ref, sem, zsem):
    dest_refs = (d0_ref, d1_ref)
    @pl.when(pl.program_id(0) == 0)
    def _():
        zero_ref[...] = jnp.zeros_like(zero_ref)
        for zb_ref, zv_ref in ((zb_e_ref, zv_e_ref), (zb_t_ref, zv_t_ref)):
            for j in range(N_EXPERTS):
                @pl.when(zv_ref[j] != 0)
                def _():
                    row0 = pl.multiple_of(zb_ref[j] * MOE_BLOCK, MOE_BLOCK)
                    cp = pltpu.make_async_copy(zero_ref, xg_hbm.at[pl.ds(row0, MOE_BLOCK)], zsem)
                    cp.start()
                    cp.wait()

    def row_copy(t, k):
        return pltpu.make_async_copy(hn_ref.at[pl.ds(t, 1)],
                                     xg_hbm.at[pl.ds(dest_refs[k][t], 1)], sem)

    def issue(t, carry):
        for k in range(TOP_K):
            row_copy(t, k).start()
        return carry

    lax.fori_loop(0, TD, issue, 0, unroll=DMA_UNROLL)

    def drain(t, carry):
        for k in range(TOP_K):
            row_copy(t, k).wait()
        return carry

    lax.fori_loop(0, TD, drain, 0, unroll=DMA_UNROLL)


def _dispatch(zero_plan, dest, hn, rows):
    t, d = hn.shape
    grid_spec = pltpu.PrefetchScalarGridSpec(
        num_scalar_prefetch=len(zero_plan),
        grid=(t // TD,),
        in_specs=[
            pl.BlockSpec((TD,), lambda i, *_: (i,), memory_space=pltpu.SMEM),
            pl.BlockSpec((TD,), lambda i, *_: (i,), memory_space=pltpu.SMEM),
            pl.BlockSpec((TD, d), lambda i, *_: (i, 0)),
        ],
        out_specs=pl.BlockSpec(memory_space=pl.ANY),
        scratch_shapes=[pltpu.VMEM((MOE_BLOCK, d), F32), pltpu.SemaphoreType.DMA,
                        pltpu.SemaphoreType.DMA],
    )
    return pl.pallas_call(
        _dispatch_kernel,
        grid_spec=grid_spec,
        out_shape=jax.ShapeDtypeStruct((rows, d), F32),
        compiler_params=pltpu.CompilerParams(
            dimension_semantics=("arbitrary",), has_side_effects=True,
            vmem_limit_bytes=VMEM_LIMIT),
        name="moe_dispatch",
    )(*zero_plan, dest[0], dest[1], hn)


def _expert_kernel(nact_ref, slot_ref, enext_ref, cs_ref, ce_ref, first_ref,
                   x_ref, wg_hbm, wu_hbm, wd_hbm, o_ref, wgb, wub, wdb, sgu, sd, sem):
    i = pl.program_id(0)
    slot = slot_ref[i]
    cs = cs_ref[i]
    ce = ce_ref[i]
    enext = enext_ref[i]

    def unit_copy(u, e):
        m, r = divmod(u, UNITS_PER_MAT)
        p = u % 2
        cols = slice(r * CAST_UNIT, (r + 1) * CAST_UNIT)
        if m == 0:
            return pltpu.make_async_copy(wg_hbm.at[e, :, cols], sgu.at[p], sem.at[p])
        if m == 1:
            return pltpu.make_async_copy(wu_hbm.at[e, :, cols], sgu.at[p], sem.at[p])
        return pltpu.make_async_copy(wd_hbm.at[e, cols, :], sd.at[p], sem.at[p])

    def unit_convert(u, s):
        m, r = divmod(u, UNITS_PER_MAT)
        p = u % 2
        cols = slice(r * CAST_UNIT, (r + 1) * CAST_UNIT)
        if m == 0:
            wgb[s, :, cols] = sgu[p].astype(BF16)
        elif m == 1:
            wub[s, :, cols] = sgu[p].astype(BF16)
        else:
            wdb[s, cols, :] = sd[p].astype(BF16)

    @pl.when(i == 0)
    def _():
        e0 = first_ref[0]
        unit_copy(0, e0).start()
        for u in range(N_UNITS):
            if u + 1 < N_UNITS:
                unit_copy(u + 1, e0).start()
            unit_copy(u, e0).wait()
            unit_convert(u, slot)

    def for_units(lo, hi, body):
        for g0 in range(0, N_UNITS, UNIT_GROUP):
            @pl.when((g0 < hi) & (g0 + UNIT_GROUP > lo))
            def _():
                for u in range(g0, g0 + UNIT_GROUP):
                    @pl.when((u >= lo) & (u < hi))
                    def _():
                        body(u)

    for_units(cs, jnp.minimum(ce, cs + 2), lambda u: unit_copy(u, enext).start())

    @pl.when(i < nact_ref[0])
    def _():
        xb = x_ref[...].astype(BF16)
        acc = jnp.zeros((MOE_BLOCK, D_MODEL), F32)
        for c in range(D_EXPERT // EXP_CHUNK):
            cols = slice(c * EXP_CHUNK, (c + 1) * EXP_CHUNK)
            g = _dot(xb, wgb[slot, :, cols])
            u = _dot(xb, wub[slot, :, cols])
            acc = acc + _dot((_silu(g) * u).astype(BF16), wdb[slot, cols, :])
        o_ref[...] = acc

    @pl.when(i >= nact_ref[0])
    def _():
        o_ref[...] = jnp.zeros_like(o_ref)

    def finish_unit(u):
        unit_copy(u, enext).wait()
        unit_convert(u, 1 - slot)
        if u + 2 < N_UNITS:
            @pl.when(u + 2 < ce)
            def _():
                unit_copy(u + 2, enext).start()

    for_units(cs, ce, finish_unit)


def _experts(plan, xg, wg, wu, wd):
    rows, d = xg.shape
    nblk = rows // MOE_BLOCK

    def row_map(i, na, *_):
        return (jnp.minimum(i, na[0] - 1), 0)

    grid_spec = pltpu.PrefetchScalarGridSpec(
        num_scalar_prefetch=len(plan),
        grid=(nblk,),
        in_specs=[
            pl.BlockSpec((MOE_BLOCK, d), row_map),
            pl.BlockSpec(memory_space=pl.ANY),
            pl.BlockSpec(memory_space=pl.ANY),
            pl.BlockSpec(memory_space=pl.ANY),
        ],
        out_specs=pl.BlockSpec((MOE_BLOCK, d), lambda i, *_: (i, 0)),
        scratch_shapes=[
            pltpu.VMEM((2, d, D_EXPERT), BF16),
            pltpu.VMEM((2, d, D_EXPERT), BF16),
            pltpu.VMEM((2, D_EXPERT, d), BF16),
            pltpu.VMEM((2, d, CAST_UNIT), F32),
            pltpu.VMEM((2, CAST_UNIT, d), F32),
            pltpu.SemaphoreType.DMA((2,)),
        ],
    )
    return pl.pallas_call(
        _expert_kernel,
        grid_spec=grid_spec,
        out_shape=jax.ShapeDtypeStruct((rows, d), F32),
        compiler_params=pltpu.CompilerParams(
            dimension_semantics=("arbitrary",), vmem_limit_bytes=VMEM_LIMIT),
        name="moe_experts",
    )(*plan, xg, wg, wu, wd)


def _combine_kernel(d0_ref, d1_ref, n0_ref, n1_ref, h_ref, gate_ref, fn_ref, yb_hbm, o_ref, buf_ref,
                    sem):
    dest_ref = (d0_ref, d1_ref)
    dnext_ref = (n0_ref, n1_ref)
    i = pl.program_id(0)
    slot = i % 2

    def row_copy(idx_ref, s, t, k):
        return pltpu.make_async_copy(yb_hbm.at[pl.ds(idx_ref[k][t], 1)],
                                     buf_ref.at[s, k, pl.ds(t, 1)], sem.at[s])

    def issue_all(idx_ref, s):
        def issue(t, carry):
            for k in range(TOP_K):
                row_copy(idx_ref, s, t, k).start()
            return carry

        lax.fori_loop(0, TC, issue, 0, unroll=DMA_UNROLL)

    @pl.when(i == 0)
    def _():
        issue_all(dest_ref, 0)

    @pl.when(i + 1 < pl.num_programs(0))
    def _():
        issue_all(dnext_ref, 1 - slot)

    def drain(t, carry):
        for k in range(TOP_K):
            row_copy(dest_ref, slot, t, k).wait()
        return carry

    lax.fori_loop(0, TC, drain, 0, unroll=DMA_UNROLL)

    gates = gate_ref[...]
    h = h_ref[...] + (gates[:, 0:1] * buf_ref[slot, 0] + gates[:, 1:2] * buf_ref[slot, 1])
    o_ref[...] = _rms(h, fn_ref[...])


def _combine(dest, h, gates, fn, yb):
    t, d = h.shape
    last = t // TC - 1
    return pl.pallas_call(
        _combine_kernel,
        grid=(t // TC,),
        in_specs=[
            pl.BlockSpec((TC,), lambda i: (i,), memory_space=pltpu.SMEM),
            pl.BlockSpec((TC,), lambda i: (i,), memory_space=pltpu.SMEM),
            pl.BlockSpec((TC,), lambda i: (jnp.minimum(i + 1, last),), memory_space=pltpu.SMEM),
            pl.BlockSpec((TC,), lambda i: (jnp.minimum(i + 1, last),), memory_space=pltpu.SMEM),
            pl.BlockSpec((TC, d), lambda i: (i, 0)),
            pl.BlockSpec((TC, TOP_K), lambda i: (i, 0)),
            pl.BlockSpec((1, d), lambda i: (0, 0)),
            pl.BlockSpec(memory_space=pl.ANY),
        ],
        out_specs=pl.BlockSpec((TC, d), lambda i: (i, 0)),
        out_shape=jax.ShapeDtypeStruct((t, d), F32),
        scratch_shapes=[pltpu.VMEM((2, TOP_K, TC, d), F32), pltpu.SemaphoreType.DMA((2,))],
        compiler_params=pltpu.CompilerParams(
            dimension_semantics=("arbitrary",), vmem_limit_bytes=VMEM_LIMIT),
        name="moe_combine",
    )(dest[0], dest[1], dest[0], dest[1], h, gates, fn, yb)


def _retention_tables(s):
    half = RET_DK // 2
    pos = jnp.arange(s, dtype=F32)
    freqs = ROPE_BASE ** (-jnp.arange(half, dtype=F32) / half)
    ang = pos[:, None] * freqs[None, :]
    cos, sin = jnp.cos(ang), jnp.sin(ang)
    cosf = jnp.concatenate([cos, cos], axis=-1)
    sins = jnp.concatenate([-sin, sin], axis=-1)
    log_g = jnp.log1p(-jnp.power(2.0, -5.0 - jnp.arange(RET_HEADS, dtype=F32)))
    idx = jnp.arange(RET_L, dtype=F32)
    ck = jnp.arange(RET_L) // CHUNK
    mask = (ck[:, None] >= ck[None, :]).astype(F32)
    dmat = jnp.exp(log_g[:, None, None] * jnp.abs(idx[:, None] - idx[None, :])) * mask[None]
    qdec = jnp.exp(log_g[:, None] * (idx + 1.0))
    kdec = jnp.exp(log_g[:, None] * (RET_L - 1.0 - idx))
    cdec = jnp.exp(log_g * RET_L)
    qdec = jnp.broadcast_to(qdec[:, :, None], (RET_HEADS, RET_L, RET_DK))
    kdec = jnp.broadcast_to(kdec[:, :, None], (RET_HEADS, RET_L, RET_DK))
    cdec = jnp.broadcast_to(cdec[:, None, None], (RET_HEADS, 1, RET_DV))
    return cosf, sins, dmat, qdec, kdec, cdec


def _moe_plan(meta_t, counts):
    t = meta_t.shape[1]
    counts = counts.astype(jnp.int32)
    pcounts = (counts + MOE_BLOCK - 1) // MOE_BLOCK * MOE_BLOCK
    pend = jnp.cumsum(pcounts)
    pstart = pend - pcounts
    experts = meta_t[0:TOP_K]
    base = jnp.zeros_like(experts)
    for e in range(N_EXPERTS):
        base = jnp.where(experts == e, pstart[e], base)
    dest = base + meta_t[TOP_K:2 * TOP_K]
    rows = (t * TOP_K + MOE_BLOCK - 1) // MOE_BLOCK * MOE_BLOCK + N_EXPERTS * MOE_BLOCK
    nblk = rows // MOE_BLOCK
    nact = pend[-1] // MOE_BLOCK
    blk = jnp.minimum(jnp.arange(nblk, dtype=jnp.int32), nact - 1)
    blk_e = jnp.sum((pend[None, :] <= (blk * MOE_BLOCK)[:, None]).astype(jnp.int32), axis=1)
    blk_e = jnp.minimum(blk_e, N_EXPERTS - 1)
    tail = nact + jnp.arange(N_EXPERTS, dtype=jnp.int32)
    zero_plan = (jnp.maximum(pend // MOE_BLOCK - 1, 0), pcounts > counts,
                 jnp.minimum(tail, nblk - 1), tail < nblk)
    i32 = lambda v: v.astype(jnp.int32)
    nb = pcounts // MOE_BLOCK
    active = nb > 0
    eids = jnp.arange(N_EXPERTS, dtype=jnp.int32)
    later = active[None, :] & (eids[None, :] > eids[:, None])
    nxt = jnp.min(jnp.where(later, eids[None, :], N_EXPERTS), axis=1)
    order = jnp.cumsum(i32(active)) - 1
    j = blk - (pstart // MOE_BLOCK)[blk_e]
    n = jnp.maximum(nb[blk_e], 1)
    live = (nxt[blk_e] < N_EXPERTS) & (jnp.arange(nblk) < nact)
    cs = jnp.where(live, (N_UNITS * j) // n, 0)
    ce = jnp.where(live, (N_UNITS * (j + 1)) // n, 0)
    enext = jnp.minimum(nxt[blk_e], N_EXPERTS - 1)
    slot = order[blk_e] % 2
    first = jnp.min(jnp.where(active, eids, N_EXPERTS - 1)).reshape(1)
    expert_plan = (i32(nact.reshape(1)), i32(slot), i32(enext), i32(cs), i32(ce), i32(first))
    return i32(dest), expert_plan, tuple(i32(v) for v in zero_plan), rows


def kernel(x, e_norm_mix, e_w_in, e_ret_gn, e_dw_w, e_dw_b, e_cv_ln_g, e_cv_ln_b, e_w_out, e_norm_ffn, e_ffn_gate, e_ffn_up, e_ffn_down, o_norm_mix, o_w_in, o_sg_ln_g, o_sg_ln_b, o_sg_w, o_sg_b, o_sc_w, o_w_out, o_norm_ffn, o_router, o_exp_gate, o_exp_up, o_exp_down, final_norm):
    bn, s, d = x.shape
    t = bn * s
    row = lambda v: v.reshape(1, -1)
    cosf, sins, dmat, qdec, kdec, cdec = _retention_tables(s)

    h = _l0_mixer(x, row(e_norm_mix[0]), e_w_in[0].astype(BF16), cosf, sins, dmat, qdec, kdec, cdec,
                  row(e_ret_gn[0]), e_dw_w[0], row(e_dw_b[0]), row(e_cv_ln_g[0]), row(e_cv_ln_b[0]),
                  e_w_out[0].astype(BF16))
    h = _ffn(h.reshape(t, d), row(e_norm_ffn[0]), e_ffn_gate[0].astype(BF16),
             e_ffn_up[0].astype(BF16), e_ffn_down[0].astype(BF16))

    sgb = jnp.broadcast_to(o_sg_b[0][:, :, None], (SG_GROUPS, SG_BLOCK, SG_CH // SG_GROUPS))
    h = _l1_mixer(h.reshape(bn, s, d), row(o_norm_mix[0]), o_w_in[0].astype(BF16),
                  row(o_sg_ln_g[0]), row(o_sg_ln_b[0]), o_sg_w[0], sgb, o_sc_w[0],
                  o_w_out[0].astype(BF16)).reshape(t, d)

    rw = jnp.pad(o_router[0], ((0, 0), (0, LANES - N_EXPERTS)))
    hn, meta, gates, counts = _router(h, row(o_norm_ffn[0]), rw)
    dest, expert_plan, zero_plan, rows = _moe_plan(meta, counts[0, :N_EXPERTS])
    xg = _dispatch(zero_plan, dest, hn, rows)
    yb = _experts(expert_plan, xg, o_exp_gate[0], o_exp_up[0], o_exp_down[0])
    out = _combine(dest, h, gates, row(final_norm), yb)
    return out.reshape(bn, s, d)
```

```python
import functools

import jax
import jax.numpy as jnp
from jax import lax
from jax.experimental import pallas as pl
from jax.experimental.pallas import tpu as pltpu

D_MODEL = 1024
CHUNK = 64
EPS = 1e-6
ROPE_BASE = 10000.0
RET_HEADS = 4
RET_DK = 128
RET_DV = 128
RET_WIDTH = RET_HEADS * RET_DV
CONV_CH = 512
CONV_K = 31
SG_CH = 512
SG_GROUPS = 4
SG_BLOCK = 128
SC_CH = 512
SC_K = 3
D_FF = 2816
N_EXPERTS = 8
TOP_K = 2
D_EXPERT = 3584
EVEN_IN = 2 * RET_HEADS * RET_DK + 2 * RET_WIDTH + 2 * CONV_CH
ODD_IN = 2 * SG_CH + 3 * SC_CH

LANES = 128
SUBLANES = 8
TM = 512
RET_L = 256
CONV_HALO = 32
CONV_ROWS = 64
SC_HALO = 8
FF_CHUNK = 2816
TR = 512
TD = 1024
DMA_UNROLL = 8
TC = 512
MOE_BLOCK = 256
EXP_CHUNK = 1792
CAST_UNIT = 256
UNITS_PER_MAT = D_EXPERT // CAST_UNIT
N_UNITS = 3 * UNITS_PER_MAT
UNIT_GROUP = 7
VMEM_LIMIT = 60 * 1024 * 1024

BF16 = jnp.bfloat16
F32 = jnp.float32


def _dot(a, b):
    return jnp.dot(a, b, preferred_element_type=F32)


def _sigmoid(x):
    return 1.0 / (1.0 + jnp.exp(-x))


def _silu(x):
    return x * _sigmoid(x)


def _rms(x, g):
    return x * lax.rsqrt(jnp.mean(x * x, axis=-1, keepdims=True) + EPS) * g


def _ln(x, g, b):
    mu = jnp.mean(x, axis=-1, keepdims=True)
    xc = x - mu
    var = jnp.mean(xc * xc, axis=-1, keepdims=True)
    return xc * lax.rsqrt(var + EPS) * g + b


def _l0_mixer_kernel(x_ref, gn_ref, win_ref, cos_ref, sin_ref, dmat_ref, qdec_ref, kdec_ref,
                     cdec_ref, retgn_ref, dww_ref, dwb_ref, lng_ref, lnb_ref, wout_ref,
                     o_ref, state_ref, cbuf_ref, z_ref, zc_ref, ycat_ref):
    @pl.when(pl.program_id(1) == 0)
    def _():
        state_ref[...] = jnp.zeros_like(state_ref)
        cbuf_ref[0:CONV_HALO, :] = jnp.zeros((CONV_HALO, CONV_CH), F32)

    x = x_ref[0]
    hn = _rms(x, gn_ref[...]).astype(BF16)
    nq = RET_HEADS * RET_DK
    n_ret = 2 * nq + 2 * RET_WIDTH
    zc_ref[...] = _dot(hn, win_ref[:, n_ret:])
    z_ref[...] = _dot(hn, win_ref[:, :n_ret])

    cbuf_ref[CONV_HALO:CONV_HALO + TM, :] = zc_ref[:, :CONV_CH] * _sigmoid(zc_ref[:, CONV_CH:])
    first = CONV_HALO - (CONV_K - 1)
    win = CONV_ROWS + CONV_HALO
    for r0 in range(0, TM, CONV_ROWS):
        accs = []
        for l in range(CONV_CH // LANES):
            lanes = slice(l * LANES, (l + 1) * LANES)
            big = cbuf_ref[r0:r0 + win, lanes]
            acc = jnp.zeros((CONV_ROWS, LANES), F32)
            for res in range(SUBLANES):
                rolled = big if res == 0 else pltpu.roll(big, win - res, axis=0)
                for a in range(win // SUBLANES):
                    t = SUBLANES * a + res - first
                    if 0 <= t < CONV_K:
                        acc = acc + rolled[SUBLANES * a:SUBLANES * a + CONV_ROWS, :] * dww_ref[t:t + 1, lanes]
            accs.append(acc)
        cv = _ln(jnp.concatenate(accs, axis=-1) + dwb_ref[...], lng_ref[...], lnb_ref[...])
        ycat_ref[r0:r0 + CONV_ROWS, RET_WIDTH:] = _silu(cv).astype(BF16)
    cbuf_ref[0:CONV_HALO, :] = cbuf_ref[TM:TM + CONV_HALO, :]

    scale = RET_DK ** -0.5
    for j in range(TM // RET_L):
        rows = slice(j * RET_L, (j + 1) * RET_L)
        cosf = cos_ref[rows, :]
        sins = sin_ref[rows, :]
        for h in range(RET_HEADS):
            cols = slice(h * RET_DK, (h + 1) * RET_DK)
            q = z_ref[rows, h * RET_DK:(h + 1) * RET_DK]
            k = z_ref[rows, nq + h * RET_DK:nq + (h + 1) * RET_DK]
            v = z_ref[rows, 2 * nq + h * RET_DV:2 * nq + (h + 1) * RET_DV].astype(BF16)
            g = z_ref[rows, 2 * nq + RET_WIDTH + h * RET_DV:2 * nq + RET_WIDTH + (h + 1) * RET_DV]
            q = (q * cosf + pltpu.roll(q, RET_DK // 2, axis=1) * sins) * scale
            k = k * cosf + pltpu.roll(k, RET_DK // 2, axis=1) * sins
            qb = q.astype(BF16)
            kb = k.astype(BF16)
            sc = lax.dot_general(qb, kb, (((1,), (1,)), ((), ())),
                                 preferred_element_type=F32) * dmat_ref[h]
            st = state_ref[h]
            o = _dot(sc.astype(BF16), v) + _dot((q * qdec_ref[h]).astype(BF16), st.astype(BF16))
            kd_t = (k * kdec_ref[h]).T.astype(BF16)
            state_ref[h] = cdec_ref[h] * st + _dot(kd_t, v)
            mu = jnp.mean(o, axis=-1, keepdims=True)
            oc = o - mu
            var = jnp.mean(oc * oc, axis=-1, keepdims=True)
            r = oc * lax.rsqrt(var + EPS) * retgn_ref[:, cols]
            ycat_ref[rows, cols] = (_silu(g) * r).astype(BF16)

    o_ref[0] = x + _dot(ycat_ref[...], wout_ref[...])


def _l0_mixer(x, gn, win, cosf, sins, dmat, qdec, kdec, cdec, retgn, dww, dwb, lng, lnb, wout):
    bn, s, d = x.shape
    full = lambda shape: pl.BlockSpec(shape, lambda b, i: (0,) * len(shape))
    return pl.pallas_call(
        _l0_mixer_kernel,
        grid=(bn, s // TM),
        in_specs=[
            pl.BlockSpec((1, TM, d), lambda b, i: (b, i, 0)),
            full((1, d)),
            full((d, EVEN_IN)),
            pl.BlockSpec((TM, RET_DK), lambda b, i: (i, 0)),
            pl.BlockSpec((TM, RET_DK), lambda b, i: (i, 0)),
            full((RET_HEADS, RET_L, RET_L)),
            full((RET_HEADS, RET_L, RET_DK)),
            full((RET_HEADS, RET_L, RET_DK)),
            full((RET_HEADS, 1, RET_DV)),
            full((1, RET_WIDTH)),
            full((CONV_K, CONV_CH)),
            full((1, CONV_CH)),
            full((1, CONV_CH)),
            full((1, CONV_CH)),
            full((RET_WIDTH + CONV_CH, d)),
        ],
        out_specs=pl.BlockSpec((1, TM, d), lambda b, i: (b, i, 0)),
        out_shape=jax.ShapeDtypeStruct((bn, s, d), F32),
        scratch_shapes=[
            pltpu.VMEM((RET_HEADS, RET_DK, RET_DV), F32),
            pltpu.VMEM((TM + CONV_HALO, CONV_CH), F32),
            pltpu.VMEM((TM, EVEN_IN - 2 * CONV_CH), F32),
            pltpu.VMEM((TM, 2 * CONV_CH), F32),
            pltpu.VMEM((TM, RET_WIDTH + CONV_CH), BF16),
        ],
        compiler_params=pltpu.CompilerParams(
            dimension_semantics=("arbitrary", "arbitrary"), vmem_limit_bytes=VMEM_LIMIT),
        name="l0_mixer",
    )(x, gn, win, cosf, sins, dmat, qdec, kdec, cdec, retgn, dww, dwb, lng, lnb, wout)


def _ffn_kernel(h_ref, gn_ref, wg_ref, wu_ref, wd_ref, o_ref):
    h = h_ref[...]
    hn = _rms(h, gn_ref[...]).astype(BF16)
    acc = h
    for c in range(D_FF // FF_CHUNK):
        cols = slice(c * FF_CHUNK, (c + 1) * FF_CHUNK)
        g = _dot(hn, wg_ref[:, cols])
        u = _dot(hn, wu_ref[:, cols])
        acc = acc + _dot((_silu(g) * u).astype(BF16), wd_ref[cols, :])
    o_ref[...] = acc


def _ffn(h, gn, wg, wu, wd):
    t, d = h.shape
    full = lambda shape: pl.BlockSpec(shape, lambda i: (0,) * len(shape))
    return pl.pallas_call(
        _ffn_kernel,
        grid=(t // TM,),
        in_specs=[
            pl.BlockSpec((TM, d), lambda i: (i, 0)),
            full((1, d)),
            full((d, D_FF)),
            full((d, D_FF)),
            full((D_FF, d)),
        ],
        out_specs=pl.BlockSpec((TM, d), lambda i: (i, 0)),
        out_shape=jax.ShapeDtypeStruct((t, d), F32),
        compiler_params=pltpu.CompilerParams(
            dimension_semantics=("arbitrary",), vmem_limit_bytes=VMEM_LIMIT),
        name="ffn_swiglu",
    )(h, gn, wg, wu, wd)


def _l1_mixer_kernel(x_ref, gn_ref, win_ref, lng_ref, lnb_ref, sgw_ref, sgb_ref, scw_ref, wout_ref,
                     o_ref, pbuf_ref, z_ref, ycat_ref):
    @pl.when(pl.program_id(1) == 0)
    def _():
        pbuf_ref[0:SC_HALO, :] = jnp.zeros((SC_HALO, SC_CH), F32)

    x = x_ref[0]
    hn = _rms(x, gn_ref[...]).astype(BF16)
    z_ref[...] = _dot(hn, win_ref[...])

    vn = _ln(jax.nn.gelu(z_ref[:, SG_CH:2 * SG_CH]), lng_ref[...], lnb_ref[...]).astype(BF16)
    gw = SG_CH // SG_GROUPS
    pc = lax.broadcasted_iota(jnp.int32, (SG_BLOCK, SG_BLOCK), 0) // CHUNK
    qc = lax.broadcasted_iota(jnp.int32, (SG_BLOCK, SG_BLOCK), 1) // CHUNK
    for g in range(SG_GROUPS):
        cols = slice(g * gw, (g + 1) * gw)
        ws = jnp.where(pc >= qc, sgw_ref[g], 0.0).astype(BF16)
        for m in range(TM // SG_BLOCK):
            rows = slice(m * SG_BLOCK, (m + 1) * SG_BLOCK)
            sp = _dot(ws, vn[rows, cols]) + sgb_ref[g]
            u = jax.nn.gelu(z_ref[rows, g * gw:(g + 1) * gw])
            ycat_ref[rows, cols] = (u * sp).astype(BF16)

    base = 2 * SG_CH
    bg = z_ref[:, base:base + SC_CH]
    cg = z_ref[:, base + SC_CH:base + 2 * SC_CH]
    hv = z_ref[:, base + 2 * SC_CH:]
    pbuf_ref[SC_HALO:SC_HALO + TM, :] = cg * hv
    first = SC_HALO - (SC_K - 1)
    conv = jnp.zeros((TM, SC_CH), F32)
    for t in range(SC_K):
        conv = conv + pbuf_ref[first + t:first + t + TM, :] * scw_ref[t:t + 1, :]
    ycat_ref[:, SG_CH:] = (bg * conv).astype(BF16)
    pbuf_ref[0:SC_HALO, :] = pbuf_ref[TM:TM + SC_HALO, :]

    o_ref[0] = x + _dot(ycat_ref[...], wout_ref[...])


def _l1_mixer(x, gn, win, lng, lnb, sgw, sgb, scw, wout):
    bn, s, d = x.shape
    full = lambda shape: pl.BlockSpec(shape, lambda b, i: (0,) * len(shape))
    return pl.pallas_call(
        _l1_mixer_kernel,
        grid=(bn, s // TM),
        in_specs=[
            pl.BlockSpec((1, TM, d), lambda b, i: (b, i, 0)),
            full((1, d)),
            full((d, ODD_IN)),
            full((1, SG_CH)),
            full((1, SG_CH)),
            full((SG_GROUPS, SG_BLOCK, SG_BLOCK)),
            full((SG_GROUPS, SG_BLOCK, SG_CH // SG_GROUPS)),
            full((SC_K, SC_CH)),
            full((SG_CH + SC_CH, d)),
        ],
        out_specs=pl.BlockSpec((1, TM, d), lambda b, i: (b, i, 0)),
        out_shape=jax.ShapeDtypeStruct((bn, s, d), F32),
        scratch_shapes=[
            pltpu.VMEM((TM + SC_HALO, SC_CH), F32),
            pltpu.VMEM((TM, ODD_IN), F32),
            pltpu.VMEM((TM, SG_CH + SC_CH), BF16),
        ],
        compiler_params=pltpu.CompilerParams(
            dimension_semantics=("arbitrary", "arbitrary"), vmem_limit_bytes=VMEM_LIMIT),
        name="l1_mixer",
    )(x, gn, win, lng, lnb, sgw, sgb, scw, wout)


def _router_kernel(h_ref, gn_ref, rw_ref, hn_ref, meta_ref, gate_ref, cnt_ref):
    @pl.when(pl.program_id(0) == 0)
    def _():
        cnt_ref[...] = jnp.zeros_like(cnt_ref)

    hn = _rms(h_ref[...], gn_ref[...])
    hn_ref[...] = hn
    hn_hi = hn.astype(BF16)
    hn_lo = (hn - hn_hi.astype(F32)).astype(BF16)
    rw = rw_ref[...]
    rw_hi = rw.astype(BF16)
    rw_lo = (rw - rw_hi.astype(F32)).astype(BF16)
    logits = _dot(hn_hi, rw_hi) + (_dot(hn_hi, rw_lo) + _dot(hn_lo, rw_hi))

    lane = lax.broadcasted_iota(jnp.int32, (TR, LANES), 1)
    neg = jnp.float32(-jnp.inf)
    l1 = jnp.where(lane < N_EXPERTS, logits, neg)
    m1 = jnp.max(l1, axis=-1, keepdims=True)
    i1 = jnp.min(jnp.where(l1 == m1, lane, LANES), axis=-1, keepdims=True)
    l2 = jnp.where(lane == i1, neg, l1)
    m2 = jnp.max(l2, axis=-1, keepdims=True)
    i2 = jnp.min(jnp.where(l2 == m2, lane, LANES), axis=-1, keepdims=True)
    e21 = jnp.exp(m2 - m1)
    g1 = 1.0 / (1.0 + e21)
    g2 = e21 / (1.0 + e21)

    oh1 = lane == i1
    oh2 = lane == i2
    both = jnp.where(oh1 | oh2, 1.0, 0.0)
    row = lax.broadcasted_iota(jnp.int32, (TR, TR), 0)
    col = lax.broadcasted_iota(jnp.int32, (TR, TR), 1)
    lower = jnp.where(row > col, 1.0, 0.0).astype(BF16)
    prefix = _dot(lower, both.astype(BF16)) + cnt_ref[...]
    r1 = jnp.sum(jnp.where(oh1, prefix, 0.0), axis=-1, keepdims=True)
    r2 = jnp.sum(jnp.where(oh2, prefix, 0.0), axis=-1, keepdims=True)
    cnt_ref[...] = cnt_ref[...] + jnp.sum(both, axis=0, keepdims=True)

    r1i = r1.astype(jnp.int32)
    r2i = r2.astype(jnp.int32)
    meta = jnp.where(lane == 0, i1, jnp.where(lane == 1, i2, jnp.where(lane == 2, r1i, r2i)))
    meta_ref[...] = meta.T[0:SUBLANES, :]
    gate_ref[...] = jnp.where(lane == 0, g1, g2)[:, 0:2]


def _router(h, gn, rw):
    t, d = h.shape
    return pl.pallas_call(
        _router_kernel,
        grid=(t // TR,),
        in_specs=[
            pl.BlockSpec((TR, d), lambda i: (i, 0)),
            pl.BlockSpec((1, d), lambda i: (0, 0)),
            pl.BlockSpec((d, LANES), lambda i: (0, 0)),
        ],
        out_specs=[
            pl.BlockSpec((TR, d), lambda i: (i, 0)),
            pl.BlockSpec((SUBLANES, TR), lambda i: (0, i)),
            pl.BlockSpec((TR, 2), lambda i: (i, 0)),
            pl.BlockSpec((1, LANES), lambda i: (0, 0)),
        ],
        out_shape=[
            jax.ShapeDtypeStruct((t, d), F32),
            jax.ShapeDtypeStruct((SUBLANES, t), jnp.int32),
            jax.ShapeDtypeStruct((t, 2), F32),
            jax.ShapeDtypeStruct((1, LANES), F32),
        ],
        compiler_params=pltpu.CompilerParams(
            dimension_semantics=("arbitrary",), vmem_limit_bytes=VMEM_LIMIT),
        name="moe_router",
    )(h, gn, rw)


def _dispatch_kernel(zb_e_ref, zv_e_ref, zb_t_ref, zv_t_ref, d0_ref, d1_ref, hn_ref, xg_hbm,
                     zero_ref, sem, zsem):
    dest_refs = (d0_ref, d1_ref)
    @pl.when(pl.program_id(0) == 0)
    def _():
        zero_ref[...] = jnp.zeros_like(zero_ref)
        for zb_ref, zv_ref in ((zb_e_ref, zv_e_ref), (zb_t_ref, zv_t_ref)):
            for j in range(N_EXPERTS):
                @pl.when(zv_ref[j] != 0)
                def _():
                    row0 = pl.multiple_of(zb_ref[j] * MOE_BLOCK, MOE_BLOCK)
                    cp = pltpu.make_async_copy(zero_ref, xg_hbm.at[pl.ds(row0, MOE_BLOCK)], zsem)
                    cp.start()
                    cp.wait()

    def row_copy(t, k):
        return pltpu.make_async_copy(hn_ref.at[pl.ds(t, 1)],
                                     xg_hbm.at[pl.ds(dest_refs[k][t], 1)], sem)

    def issue(t, carry):
        for k in range(TOP_K):
            row_copy(t, k).start(priority=k)
        return carry

    lax.fori_loop(0, TD, issue, 0, unroll=DMA_UNROLL)

    def drain(t, carry):
        for k in range(TOP_K):
            row_copy(t, k).wait()
        return carry

    lax.fori_loop(0, TD, drain, 0, unroll=DMA_UNROLL)


def _dispatch(zero_plan, dest, hn, rows):
    t, d = hn.shape
    grid_spec = pltpu.PrefetchScalarGridSpec(
        num_scalar_prefetch=len(zero_plan),
        grid=(t // TD,),
        in_specs=[
            pl.BlockSpec((TD,), lambda i, *_: (i,), memory_space=pltpu.SMEM),
            pl.BlockSpec((TD,), lambda i, *_: (i,), memory_space=pltpu.SMEM),
            pl.BlockSpec((TD, d), lambda i, *_: (i, 0)),
        ],
        out_specs=pl.BlockSpec(memory_space=pl.ANY),
        scratch_shapes=[pltpu.VMEM((MOE_BLOCK, d), F32), pltpu.SemaphoreType.DMA,
                        pltpu.SemaphoreType.DMA],
    )
    return pl.pallas_call(
        _dispatch_kernel,
        grid_spec=grid_spec,
        out_shape=jax.ShapeDtypeStruct((rows, d), F32),
        compiler_params=pltpu.CompilerParams(
            dimension_semantics=("arbitrary",), has_side_effects=True,
            vmem_limit_bytes=VMEM_LIMIT),
        name="moe_dispatch",
    )(*zero_plan, dest[0], dest[1], hn)


def _expert_kernel(nact_ref, slot_ref, enext_ref, cs_ref, ce_ref, first_ref,
                   x_ref, wg_hbm, wu_hbm, wd_hbm, o_ref, wgb, wub, wdb, sgu, sd, sem):
    i = pl.program_id(0)
    slot = slot_ref[i]
    cs = cs_ref[i]
    ce = ce_ref[i]
    enext = enext_ref[i]

    def unit_copy(u, e):
        m, r = divmod(u, UNITS_PER_MAT)
        p = u % 2
        cols = slice(r * CAST_UNIT, (r + 1) * CAST_UNIT)
        if m == 0:
            return pltpu.make_async_copy(wg_hbm.at[e, :, cols], sgu.at[p], sem.at[p])
        if m == 1:
            return pltpu.make_async_copy(wu_hbm.at[e, :, cols], sgu.at[p], sem.at[p])
        return pltpu.make_async_copy(wd_hbm.at[e, cols, :], sd.at[p], sem.at[p])

    def unit_convert(u, s):
        m, r = divmod(u, UNITS_PER_MAT)
        p = u % 2
        cols = slice(r * CAST_UNIT, (r + 1) * CAST_UNIT)
        if m == 0:
            wgb[s, :, cols] = sgu[p].astype(BF16)
        elif m == 1:
            wub[s, :, cols] = sgu[p].astype(BF16)
        else:
            wdb[s, cols, :] = sd[p].astype(BF16)

    @pl.when(i == 0)
    def _():
        e0 = first_ref[0]
        unit_copy(0, e0).start()
        for u in range(N_UNITS):
            if u + 1 < N_UNITS:
                unit_copy(u + 1, e0).start()
            unit_copy(u, e0).wait()
            unit_convert(u, slot)

    def for_units(lo, hi, body):
        for g0 in range(0, N_UNITS, UNIT_GROUP):
            @pl.when((g0 < hi) & (g0 + UNIT_GROUP > lo))
            def _():
                for u in range(g0, g0 + UNIT_GROUP):
                    @pl.when((u >= lo) & (u < hi))
                    def _():
                        body(u)

    for_units(cs, jnp.minimum(ce, cs + 2), lambda u: unit_copy(u, enext).start())

    @pl.when(i < nact_ref[0])
    def _():
        xb = x_ref[...].astype(BF16)
        acc = jnp.zeros((MOE_BLOCK, D_MODEL), F32)
        for c in range(D_EXPERT // EXP_CHUNK):
            cols = slice(c * EXP_CHUNK, (c + 1) * EXP_CHUNK)
            g = _dot(xb, wgb[slot, :, cols])
            u = _dot(xb, wub[slot, :, cols])
            acc = acc + _dot((_silu(g) * u).astype(BF16), wdb[slot, cols, :])
        o_ref[...] = acc

    @pl.when(i >= nact_ref[0])
    def _():
        o_ref[...] = jnp.zeros_like(o_ref)

    def finish_unit(u):
        unit_copy(u, enext).wait()
        unit_convert(u, 1 - slot)
        if u + 2 < N_UNITS:
            @pl.when(u + 2 < ce)
            def _():
                unit_copy(u + 2, enext).start()

    for_units(cs, ce, finish_unit)


def _experts(plan, xg, wg, wu, wd):
    rows, d = xg.shape
    nblk = rows // MOE_BLOCK

    def row_map(i, na, *_):
        return (jnp.minimum(i, na[0] - 1), 0)

    grid_spec = pltpu.PrefetchScalarGridSpec(
        num_scalar_prefetch=len(plan),
        grid=(nblk,),
        in_specs=[
            pl.BlockSpec((MOE_BLOCK, d), row_map),
            pl.BlockSpec(memory_space=pl.ANY),
            pl.BlockSpec(memory_space=pl.ANY),
            pl.BlockSpec(memory_space=pl.ANY),
        ],
        out_specs=pl.BlockSpec((MOE_BLOCK, d), lambda i, *_: (i, 0)),
        scratch_shapes=[
            pltpu.VMEM((2, d, D_EXPERT), BF16),
            pltpu.VMEM((2, d, D_EXPERT), BF16),
            pltpu.VMEM((2, D_EXPERT, d), BF16),
            pltpu.VMEM((2, d, CAST_UNIT), F32),
            pltpu.VMEM((2, CAST_UNIT, d), F32),
            pltpu.SemaphoreType.DMA((2,)),
        ],
    )
    return pl.pallas_call(
        _expert_kernel,
        grid_spec=grid_spec,
        out_shape=jax.ShapeDtypeStruct((rows, d), F32),
        compiler_params=pltpu.CompilerParams(
            dimension_semantics=("arbitrary",), vmem_limit_bytes=VMEM_LIMIT),
        name="moe_experts",
    )(*plan, xg, wg, wu, wd)


def _combine_kernel(d0_ref, d1_ref, n0_ref, n1_ref, h_ref, gate_ref, fn_ref, yb_hbm, o_ref, buf_ref,
                    sem):
    dest_ref = (d0_ref, d1_ref)
    dnext_ref = (n0_ref, n1_ref)
    i = pl.program_id(0)
    slot = i % 2

    def row_copy(idx_ref, s, t, k):
        return pltpu.make_async_copy(yb_hbm.at[pl.ds(idx_ref[k][t], 1)],
                                     buf_ref.at[s, k, pl.ds(t, 1)], sem.at[s])

    def issue_all(idx_ref, s):
        def issue(t, carry):
            for k in range(TOP_K):
                row_copy(idx_ref, s, t, k).start(priority=k)
            return carry

        lax.fori_loop(0, TC, issue, 0, unroll=DMA_UNROLL)

    @pl.when(i == 0)
    def _():
        issue_all(dest_ref, 0)

    @pl.when(i + 1 < pl.num_programs(0))
    def _():
        issue_all(dnext_ref, 1 - slot)

    def drain(t, carry):
        for k in range(TOP_K):
            row_copy(dest_ref, slot, t, k).wait()
        return carry

    lax.fori_loop(0, TC, drain, 0, unroll=DMA_UNROLL)

    gates = gate_ref[...]
    h = h_ref[...] + (gates[:, 0:1] * buf_ref[slot, 0] + gates[:, 1:2] * buf_ref[slot, 1])
    o_ref[...] = _rms(h, fn_ref[...])


def _combine(dest, h, gates, fn, yb):
    t, d = h.shape
    last = t // TC - 1
    return pl.pallas_call(
        _combine_kernel,
        grid=(t // TC,),
        in_specs=[
            pl.BlockSpec((TC,), lambda i: (i,), memory_space=pltpu.SMEM),
            pl.BlockSpec((TC,), lambda i: (i,), memory_space=pltpu.SMEM),
            pl.BlockSpec((TC,), lambda i: (jnp.minimum(i + 1, last),), memory_space=pltpu.SMEM),
            pl.BlockSpec((TC,), lambda i: (jnp.minimum(i + 1, last),), memory_space=pltpu.SMEM),
            pl.BlockSpec((TC, d), lambda i: (i, 0)),
            pl.BlockSpec((TC, TOP_K), lambda i: (i, 0)),
            pl.BlockSpec((1, d), lambda i: (0, 0)),
            pl.BlockSpec(memory_space=pl.ANY),
        ],
        out_specs=pl.BlockSpec((TC, d), lambda i: (i, 0)),
        out_shape=jax.ShapeDtypeStruct((t, d), F32),
        scratch_shapes=[pltpu.VMEM((2, TOP_K, TC, d), F32), pltpu.SemaphoreType.DMA((2,))],
        compiler_params=pltpu.CompilerParams(
            dimension_semantics=("arbitrary",), vmem_limit_bytes=VMEM_LIMIT),
        name="moe_combine",
    )(dest[0], dest[1], dest[0], dest[1], h, gates, fn, yb)


def _retention_tables(s):
    half = RET_DK // 2
    pos = jnp.arange(s, dtype=F32)
    freqs = ROPE_BASE ** (-jnp.arange(half, dtype=F32) / half)
    ang = pos[:, None] * freqs[None, :]
    cos, sin = jnp.cos(ang), jnp.sin(ang)
    cosf = jnp.concatenate([cos, cos], axis=-1)
    sins = jnp.concatenate([-sin, sin], axis=-1)
    log_g = jnp.log1p(-jnp.power(2.0, -5.0 - jnp.arange(RET_HEADS, dtype=F32)))
    idx = jnp.arange(RET_L, dtype=F32)
    ck = jnp.arange(RET_L) // CHUNK
    mask = (ck[:, None] >= ck[None, :]).astype(F32)
    dmat = jnp.exp(log_g[:, None, None] * jnp.abs(idx[:, None] - idx[None, :])) * mask[None]
    qdec = jnp.exp(log_g[:, None] * (idx + 1.0))
    kdec = jnp.exp(log_g[:, None] * (RET_L - 1.0 - idx))
    cdec = jnp.exp(log_g * RET_L)
    qdec = jnp.broadcast_to(qdec[:, :, None], (RET_HEADS, RET_L, RET_DK))
    kdec = jnp.broadcast_to(kdec[:, :, None], (RET_HEADS, RET_L, RET_DK))
    cdec = jnp.broadcast_to(cdec[:, None, None], (RET_HEADS, 1, RET_DV))
    return cosf, sins, dmat, qdec, kdec, cdec


def _moe_plan(meta_t, counts):
    t = meta_t.shape[1]
    counts = counts.astype(jnp.int32)
    pcounts = (counts + MOE_BLOCK - 1) // MOE_BLOCK * MOE_BLOCK
    pend = jnp.cumsum(pcounts)
    pstart = pend - pcounts
    experts = meta_t[0:TOP_K]
    base = jnp.zeros_like(experts)
    for e in range(N_EXPERTS):
        base = jnp.where(experts == e, pstart[e], base)
    dest = base + meta_t[TOP_K:2 * TOP_K]
    rows = (t * TOP_K + MOE_BLOCK - 1) // MOE_BLOCK * MOE_BLOCK + N_EXPERTS * MOE_BLOCK
    nblk = rows // MOE_BLOCK
    nact = pend[-1] // MOE_BLOCK
    blk = jnp.minimum(jnp.arange(nblk, dtype=jnp.int32), nact - 1)
    blk_e = jnp.sum((pend[None, :] <= (blk * MOE_BLOCK)[:, None]).astype(jnp.int32), axis=1)
    blk_e = jnp.minimum(blk_e, N_EXPERTS - 1)
    tail = nact + jnp.arange(N_EXPERTS, dtype=jnp.int32)
    zero_plan = (jnp.maximum(pend // MOE_BLOCK - 1, 0), pcounts > counts,
                 jnp.minimum(tail, nblk - 1), tail < nblk)
    i32 = lambda v: v.astype(jnp.int32)
    nb = pcounts // MOE_BLOCK
    active = nb > 0
    eids = jnp.arange(N_EXPERTS, dtype=jnp.int32)
    later = active[None, :] & (eids[None, :] > eids[:, None])
    nxt = jnp.min(jnp.where(later, eids[None, :], N_EXPERTS), axis=1)
    order = jnp.cumsum(i32(active)) - 1
    j = blk - (pstart // MOE_BLOCK)[blk_e]
    n = jnp.maximum(nb[blk_e], 1)
    live = (nxt[blk_e] < N_EXPERTS) & (jnp.arange(nblk) < nact)
    cs = jnp.where(live, (N_UNITS * j) // n, 0)
    ce = jnp.where(live, (N_UNITS * (j + 1)) // n, 0)
    enext = jnp.minimum(nxt[blk_e], N_EXPERTS - 1)
    slot = order[blk_e] % 2
    first = jnp.min(jnp.where(active, eids, N_EXPERTS - 1)).reshape(1)
    expert_plan = (i32(nact.reshape(1)), i32(slot), i32(enext), i32(cs), i32(ce), i32(first))
    return i32(dest), expert_plan, tuple(i32(v) for v in zero_plan), rows


def kernel(x, e_norm_mix, e_w_in, e_ret_gn, e_dw_w, e_dw_b, e_cv_ln_g, e_cv_ln_b, e_w_out, e_norm_ffn, e_ffn_gate, e_ffn_up, e_ffn_down, o_norm_mix, o_w_in, o_sg_ln_g, o_sg_ln_b, o_sg_w, o_sg_b, o_sc_w, o_w_out, o_norm_ffn, o_router, o_exp_gate, o_exp_up, o_exp_down, final_norm):
    bn, s, d = x.shape
    t = bn * s
    row = lambda v: v.reshape(1, -1)
    cosf, sins, dmat, qdec, kdec, cdec = _retention_tables(s)

    h = _l0_mixer(x, row(e_norm_mix[0]), e_w_in[0].astype(BF16), cosf, sins, dmat, qdec, kdec, cdec,
                  row(e_ret_gn[0]), e_dw_w[0], row(e_dw_b[0]), row(e_cv_ln_g[0]), row(e_cv_ln_b[0]),
                  e_w_out[0].astype(BF16))
    h = _ffn(h.reshape(t, d), row(e_norm_ffn[0]), e_ffn_gate[0].astype(BF16),
             e_ffn_up[0].astype(BF16), e_ffn_down[0].astype(BF16))

    sgb = jnp.broadcast_to(o_sg_b[0][:, :, None], (SG_GROUPS, SG_BLOCK, SG_CH // SG_GROUPS))
    h = _l1_mixer(h.reshape(bn, s, d), row(o_norm_mix[0]), o_w_in[0].astype(BF16),
                  row(o_sg_ln_g[0]), row(o_sg_ln_b[0]), o_sg_w[0], sgb, o_sc_w[0],
                  o_w_out[0].astype(BF16)).reshape(t, d)

    rw = jnp.pad(o_router[0], ((0, 0), (0, LANES - N_EXPERTS)))
    hn, meta, gates, counts = _router(h, row(o_norm_ffn[0]), rw)
    dest, expert_plan, zero_plan, rows = _moe_plan(meta, counts[0, :N_EXPERTS])
    xg = _dispatch(zero_plan, dest, hn, rows)
    yb = _experts(expert_plan, xg, o_exp_gate[0], o_exp_up[0], o_exp_down[0])
    out = _combine(dest, h, gates, row(final_norm), yb)
    return out.reshape(bn, s, d)
```
